```python
import math
import jax, jax.numpy as jnp
from jax import lax
import numpy as np

D_MODEL = 1024
BATCH = 2
SEQ = 8192
DEPTH = 2

N_A_LAYERS = DEPTH // 2
N_B_LAYERS = DEPTH - N_A_LAYERS

GDN_HEADS = 8
GDN_HEAD_DIM = 128
GDN_WIDTH = GDN_HEADS * GDN_HEAD_DIM
CONV_WIDTH = 4
CHUNK = 64
GDN_IN_COLS = 4 * GDN_WIDTH + 2 * GDN_HEADS

SB_HEADS = 8
SB_HEAD_DIM = 128
SB_WIDTH = SB_HEADS * SB_HEAD_DIM
Q_BLOCK = 128

D_FF = 4 * D_MODEL

EPS = 1e-6

kernel_name = "yoco_gdn_stickbreaking_hybrid"


def rms_norm(x, gain):
    xf = x.astype(jnp.float32)
    y = xf * lax.rsqrt(jnp.mean(xf * xf, axis=-1, keepdims=True) + EPS)
    return (y * gain.astype(jnp.float32)).astype(x.dtype)


def l2_norm(x):
    xf = x.astype(jnp.float32)
    return xf * lax.rsqrt(jnp.sum(xf * xf, axis=-1, keepdims=True) + EPS)


def causal_dwconv(x, w):
    k_w = w.shape[0]
    t_len = x.shape[1]
    xp = jnp.pad(x, ((0, 0), (k_w - 1, 0), (0, 0)))
    return sum(xp[:, i:i + t_len, :] * w[i] for i in range(k_w))


def gated_delta_rule_chunked(q, k, v, beta, g):
    b, t_len, h, dk = q.shape
    dv = v.shape[-1]
    n = t_len // CHUNK

    def chunks(t):
        t = jnp.moveaxis(t, 2, 1)
        return t.reshape((b, h, n, CHUNK) + t.shape[3:])

    q, k, v, beta, g = (chunks(t) for t in (q, k, v, beta, g))
    gc = jnp.cumsum(g, axis=-1)
    idx = jnp.arange(CHUNK)
    incl = idx[:, None] >= idx[None, :]
    strict = idx[:, None] > idx[None, :]
    diff = gc[..., :, None] - gc[..., None, :]
    decay = jnp.where(incl, jnp.exp(jnp.where(incl, diff, 0.0)), 0.0)

    kb = k * beta[..., None]
    lower = jnp.where(strict, jnp.einsum('bhnid,bhnjd->bhnij', kb, k) * decay, 0.0)
    eye = jnp.eye(CHUNK, dtype=jnp.float32)
    t_mat = lax.linalg.triangular_solve(eye + lower, jnp.broadcast_to(eye, lower.shape),
                                        left_side=True, lower=True)
    w = t_mat @ (kb * jnp.exp(gc)[..., None])
    u = t_mat @ (v * beta[..., None])
    attn = jnp.einsum('bhnid,bhnjd->bhnij', q, k) * decay
    qg = q * jnp.exp(gc)[..., None]
    kg = k * jnp.exp(gc[..., -1:] - gc)[..., None]
    g_last = jnp.exp(gc[..., -1])

    xs = tuple(jnp.moveaxis(t, 2, 0) for t in (qg, kg, w, u, attn, g_last))

    def step(state, inp):
        qg_c, kg_c, w_c, u_c, attn_c, gl_c = inp
        v_new = u_c - w_c @ state
        o_c = qg_c @ state + attn_c @ v_new
        state = state * gl_c[..., None, None] + jnp.einsum('bhck,bhcv->bhkv', kg_c, v_new)
        return state, o_c

    s0 = jnp.zeros((b, h, dk, dv), jnp.float32)
    _, o = lax.scan(step, s0, xs)
    o = jnp.moveaxis(o, 0, 2).reshape(b, h, t_len, dv)
    return jnp.moveaxis(o, 1, 2)


def gated_deltanet(h, w_in, conv_w, a_log, dt_bias, out_gain, w_out):
    b, t_len, _ = h.shape
    proj = h @ w_in
    qkv, gate, b_raw, a_raw = jnp.split(
        proj, [3 * GDN_WIDTH, 4 * GDN_WIDTH, 4 * GDN_WIDTH + GDN_HEADS], axis=-1)
    qkv = jax.nn.silu(causal_dwconv(qkv, conv_w))
    q, k, v = jnp.split(qkv, 3, axis=-1)

    def heads(t):
        return t.reshape(b, t_len, GDN_HEADS, GDN_HEAD_DIM).astype(jnp.float32)

    q = l2_norm(heads(q)) * (GDN_HEAD_DIM ** -0.5)
    k = l2_norm(heads(k))
    v = heads(v)
    beta = jax.nn.sigmoid(b_raw.astype(jnp.float32))
    g = -jnp.exp(a_log.astype(jnp.float32)) * jax.nn.softplus(
        a_raw.astype(jnp.float32) + dt_bias.astype(jnp.float32))
    o = gated_delta_rule_chunked(q, k, v, beta, g)
    o = rms_norm(o, out_gain) * jax.nn.silu(heads(gate))
    return o.reshape(b, t_len, GDN_WIDTH).astype(h.dtype) @ w_out


def stick_breaking_attention(q, k, v):
    b, h, t_len, d = q.shape
    nb = t_len // Q_BLOCK
    qb = jnp.moveaxis(q.reshape(b, h, nb, Q_BLOCK, d), 2, 0)
    key_pos = jnp.arange(t_len)
    scale = d ** -0.5

    def block(args):
        q_blk, i = args
        z = jnp.einsum('bhqd,bhkd->bhqk', q_blk, k).astype(jnp.float32) * scale
        q_pos = i * Q_BLOCK + jnp.arange(Q_BLOCK)
        before = key_pos[None, :] < q_pos[:, None]
        log_beta = jax.nn.log_sigmoid(z)
        log_1m = jnp.where(before, jax.nn.log_sigmoid(-z), 0.0)
        tail = lax.cumsum(log_1m, axis=3, reverse=True) - log_1m
        a = jnp.where(before, jnp.exp(log_beta + tail), 0.0)
        return jnp.einsum('bhqk,bhkd->bhqd', a.astype(v.dtype), v)

    o = lax.map(block, (qb, jnp.arange(nb)))
    return jnp.moveaxis(o, 0, 2).reshape(b, h, t_len, d)


def shared_kv(x, kv_gain, w_kv):
    b, t_len, _ = x.shape
    kv = rms_norm(x, kv_gain) @ w_kv
    k, v = jnp.split(kv, 2, axis=-1)
    k = k.reshape(b, t_len, SB_HEADS, SB_HEAD_DIM).transpose(0, 2, 1, 3)
    v = v.reshape(b, t_len, SB_HEADS, SB_HEAD_DIM).transpose(0, 2, 1, 3)
    return k, v


def stick_breaking_mixer(h, w_q, w_o, k_sh, v_sh):
    b, t_len, _ = h.shape
    q = (h @ w_q).reshape(b, t_len, SB_HEADS, SB_HEAD_DIM).transpose(0, 2, 1, 3)
    o = stick_breaking_attention(q, k_sh, v_sh)
    return o.transpose(0, 2, 1, 3).reshape(b, t_len, SB_WIDTH) @ w_o


def squared_relu_mlp(h, w_up, w_down):
    return jnp.square(jax.nn.relu(h @ w_up)) @ w_down


def setup_inputs(seed: int = 0) -> dict:
    key = jax.random.key(seed)
    ks = jax.random.split(key, 20)
    f32 = jnp.float32

    def nrm(k, shape, fan_in):
        return jax.random.normal(k, shape, f32) * (fan_in ** -0.5)

    def gain(k, shape):
        return 1.0 + 0.05 * jax.random.normal(k, shape, f32)

    x = jax.random.normal(ks[0], (BATCH, SEQ, D_MODEL), f32)
    dt = jnp.exp(jax.random.uniform(ks[10], (N_A_LAYERS, GDN_HEADS), f32,
                                    minval=math.log(1e-3), maxval=math.log(1e-1)))
    dt_bias = dt + jnp.log(-jnp.expm1(-dt))
    a_log = jnp.log(jax.random.uniform(ks[11], (N_A_LAYERS, GDN_HEADS), f32,
                                       minval=1.0, maxval=16.0))
    return {
        "x": x,
        "mix_pre_gain": gain(ks[1], (DEPTH, D_MODEL)),
        "mix_post_gain": gain(ks[2], (DEPTH, D_MODEL)),
        "mlp_pre_gain": gain(ks[3], (DEPTH, D_MODEL)),
        "mlp_post_gain": gain(ks[4], (DEPTH, D_MODEL)),
        "mlp_w_up": nrm(ks[5], (DEPTH, D_MODEL, D_FF), D_MODEL),
        "mlp_w_down": nrm(ks[6], (DEPTH, D_FF, D_MODEL), D_FF),
        "gdn_w_in": nrm(ks[7], (N_A_LAYERS, D_MODEL, GDN_IN_COLS), D_MODEL),
        "gdn_conv_w": nrm(ks[8], (N_A_LAYERS, CONV_WIDTH, 3 * GDN_WIDTH), CONV_WIDTH),
        "gdn_a_log": a_log,
        "gdn_dt_bias": dt_bias,
        "gdn_out_gain": gain(ks[12], (N_A_LAYERS, GDN_HEAD_DIM)),
        "gdn_w_out": nrm(ks[13], (N_A_LAYERS, GDN_WIDTH, D_MODEL), GDN_WIDTH),
        "kv_gain": gain(ks[14], (D_MODEL,)),
        "w_kv": nrm(ks[15], (D_MODEL, 2 * SB_WIDTH), D_MODEL),
        "sb_w_q": nrm(ks[16], (N_B_LAYERS, D_MODEL, SB_WIDTH), D_MODEL),
        "sb_w_o": nrm(ks[17], (N_B_LAYERS, SB_WIDTH, D_MODEL), SB_WIDTH),
    }


def reference(x, mix_pre_gain, mix_post_gain, mlp_pre_gain, mlp_post_gain, mlp_w_up, mlp_w_down,
              gdn_w_in, gdn_conv_w, gdn_a_log, gdn_dt_bias, gdn_out_gain, gdn_w_out,
              kv_gain, w_kv, sb_w_q, sb_w_o):
    k_sh = None
    v_sh = None
    for layer in range(DEPTH):
        h = rms_norm(x, mix_pre_gain[layer])
        if layer < N_A_LAYERS:
            a = layer
            mix = gated_deltanet(h, gdn_w_in[a], gdn_conv_w[a], gdn_a_log[a], gdn_dt_bias[a],
                                 gdn_out_gain[a], gdn_w_out[a])
        else:
            if layer == N_A_LAYERS:
                k_sh, v_sh = shared_kv(x, kv_gain, w_kv)
            bl = layer - N_A_LAYERS
            mix = stick_breaking_mixer(h, sb_w_q[bl], sb_w_o[bl], k_sh, v_sh)
        x = x + rms_norm(mix, mix_post_gain[layer])
        h = rms_norm(x, mlp_pre_gain[layer])
        x = x + rms_norm(squared_relu_mlp(h, mlp_w_up[layer], mlp_w_down[layer]), mlp_post_gain[layer])
    return x
```

```python
import functools
import math

import jax
import jax.numpy as jnp
from jax import lax
from jax.experimental import pallas as pl
from jax.experimental.pallas import tpu as pltpu

F32 = jnp.float32
BF16 = jnp.bfloat16
EPS = 1e-6

V7X_LANES = 128
V7X_SUBLANES = 8
V7X_VMEM_BYTES = 64 * 1024 * 1024
V7X_VMEM_REQUEST_CAP = 56 * 1024 * 1024

GDN_CHUNK = 64
GDN_BLOCK = 256
GDN_BASE = 8
CONV_TAPS = 4


def _vmem_limit(nbytes):
    return int(min(max(nbytes, 16 * 1024 * 1024), V7X_VMEM_REQUEST_CAP))


def _resident(shape, index_map):
    return pl.BlockSpec(shape, index_map, pipeline_mode=pl.Buffered(1))


def _rms_scale(x):
    return lax.rsqrt(jnp.mean(x * x, axis=-1, keepdims=True) + EPS)


def _dot(a, b):
    return jnp.dot(a, b, preferred_element_type=F32)


def _dot_f32(a, b):
    return jnp.dot(a, b, preferred_element_type=F32, precision=lax.Precision.HIGHEST)


def _dot_nt(a, b):
    return lax.dot_general(a, b, (((1,), (1,)), ((), ())), preferred_element_type=F32)


def _dot_tn(a, b):
    return lax.dot_general(a, b, (((0,), (0,)), ((), ())), preferred_element_type=F32)


def _norm_proj_kernel(*refs, n_groups, scales, col_chunk):
    x_ref = refs[0]
    gain_refs = refs[1:1 + n_groups]
    w_refs = refs[1 + n_groups:1 + 2 * n_groups]
    out_refs = refs[1 + 2 * n_groups:]
    x = x_ref[...]
    xn = x * _rms_scale(x)
    for g in range(n_groups):
        h = (xn * gain_refs[g][...]).astype(BF16)
        n_cols = w_refs[g].shape[1]
        step = min(col_chunk, n_cols)
        for c in range(0, n_cols, step):
            acc = _dot(h, w_refs[g][:, c:c + step])
            if scales[g] != 1.0:
                acc = acc * scales[g]
            out_refs[g][:, c:c + step] = acc.astype(out_refs[g].dtype)


def norm_proj(x, groups, *, row_tile=512, col_chunk=1024):
    n, d = x.shape
    tm = min(row_tile, n)
    assert n % tm == 0
    n_groups = len(groups)
    gains = [g[0].reshape(1, d).astype(F32) for g in groups]
    ws = [g[1] for g in groups]
    in_specs = [pl.BlockSpec((tm, d), lambda i: (i, 0))]
    in_specs += [_resident((1, d), lambda i: (0, 0)) for _ in groups]
    in_specs += [_resident(w.shape, lambda i: (0, 0)) for w in ws]
    out_specs = [pl.BlockSpec((tm, w.shape[1]), lambda i: (i, 0)) for w in ws]
    out_shape = [jax.ShapeDtypeStruct((n, w.shape[1]), g[2]) for w, g in zip(ws, groups)]
    vmem = 2 * tm * d * 4 + sum(w.size * 2 for w in ws)
    vmem += sum(2 * tm * w.shape[1] * jnp.dtype(g[2]).itemsize for w, g in zip(ws, groups))
    vmem += 4 * tm * max(d, col_chunk) * 4
    return pl.pallas_call(
        functools.partial(_norm_proj_kernel, n_groups=n_groups,
                          scales=tuple(float(g[3]) for g in groups), col_chunk=col_chunk),
        grid=(n // tm,),
        in_specs=in_specs,
        out_specs=out_specs,
        out_shape=out_shape,
        compiler_params=pltpu.CompilerParams(
            dimension_semantics=("arbitrary",), vmem_limit_bytes=_vmem_limit(vmem)),
        name="norm_proj",
    )(x, *gains, *ws)


def _proj_post_kernel(a_ref, w_ref, gain_ref, r_ref, o_ref):
    y = _dot(a_ref[...], w_ref[...])
    o_ref[...] = r_ref[...] + y * _rms_scale(y) * gain_ref[...]


def proj_post(a, w, gain, resid, *, row_tile=512):
    n, k = a.shape
    d = w.shape[1]
    tm = min(row_tile, n)
    assert n % tm == 0
    vmem = 2 * tm * k * 2 + k * d * 2 + 4 * tm * d * 4 + 2 * tm * d * 4
    return pl.pallas_call(
        _proj_post_kernel,
        grid=(n // tm,),
        in_specs=[pl.BlockSpec((tm, k), lambda i: (i, 0)),
                  _resident((k, d), lambda i: (0, 0)),
                  _resident((1, d), lambda i: (0, 0)),
                  pl.BlockSpec((tm, d), lambda i: (i, 0))],
        out_specs=pl.BlockSpec((tm, d), lambda i: (i, 0)),
        out_shape=jax.ShapeDtypeStruct((n, d), F32),
        compiler_params=pltpu.CompilerParams(
            dimension_semantics=("arbitrary",), vmem_limit_bytes=_vmem_limit(vmem)),
        name="proj_post",
    )(a, w, gain.reshape(1, d).astype(F32), resid)


def _mlp_kernel(x_ref, gpre_ref, wup_ref, wdn_ref, gpost_ref, o_ref, *, ff_chunk):
    x = x_ref[...]
    h = (x * _rms_scale(x) * gpre_ref[...]).astype(BF16)
    d_ff = wup_ref.shape[1]
    y = None
    for c in range(0, d_ff, ff_chunk):
        u = jnp.maximum(_dot(h, wup_ref[:, c:c + ff_chunk]), 0.0)
        part = _dot((u * u).astype(BF16), wdn_ref[c:c + ff_chunk, :])
        y = part if y is None else y + part
    o_ref[...] = x + y * _rms_scale(y) * gpost_ref[...]


def mlp(x, g_pre, w_up, w_down, g_post, *, row_tile=512, ff_chunk=1024):
    n, d = x.shape
    d_ff = w_up.shape[1]
    tm = min(row_tile, n)
    fc = min(ff_chunk, d_ff)
    assert n % tm == 0 and d_ff % fc == 0
    vmem = 4 * tm * d * 4 + 2 * d * d_ff * 2 + 3 * tm * fc * 4 + 2 * tm * d * 4
    return pl.pallas_call(
        functools.partial(_mlp_kernel, ff_chunk=fc),
        grid=(n // tm,),
        in_specs=[pl.BlockSpec((tm, d), lambda i: (i, 0)),
                  _resident((1, d), lambda i: (0, 0)),
                  _resident((d, d_ff), lambda i: (0, 0)),
                  _resident((d_ff, d), lambda i: (0, 0)),
                  _resident((1, d), lambda i: (0, 0))],
        out_specs=pl.BlockSpec((tm, d), lambda i: (i, 0)),
        out_shape=jax.ShapeDtypeStruct((n, d), F32),
        compiler_params=pltpu.CompilerParams(
            dimension_semantics=("arbitrary",), vmem_limit_bytes=_vmem_limit(vmem)),
        name="mlp",
    )(x, g_pre.reshape(1, d).astype(F32), w_up, w_down, g_post.reshape(1, d).astype(F32))


def _shift_rows(x, tail, s):
    xs = pltpu.roll(x, s, axis=0)
    ts = pltpu.roll(tail, s, axis=0)
    row = lax.broadcasted_iota(jnp.int32, tail.shape, 0)
    top = jnp.where(row < s, ts, xs[:V7X_SUBLANES])
    return jnp.concatenate([top, xs[V7X_SUBLANES:]], axis=0)


def _conv_silu(x_bf16, w, tail_ref):
    x = x_bf16.astype(F32)
    tail = tail_ref[...]
    acc = x * w[CONV_TAPS - 1:CONV_TAPS, :]
    for s in range(1, CONV_TAPS):
        acc = acc + _shift_rows(x, tail, s) * w[CONV_TAPS - 1 - s:CONV_TAPS - s, :]
    tail_ref[...] = x[x.shape[0] - V7X_SUBLANES:, :]
    return acc * jax.nn.sigmoid(acc)


def _softplus(x):
    return jnp.maximum(x, 0.0) + jnp.log(1.0 + jnp.exp(-jnp.abs(x)))


def _col_bcast(row, width):
    return jnp.broadcast_to(row, (width, row.shape[1])).T


def _same_block(ri, ci, size):
    shift = size.bit_length() - 1
    assert 1 << shift == size
    return (ri >> shift) == (ci >> shift)


def _inv_unit_lower(low, ri, ci):
    eye = (ri == ci).astype(F32)
    base = jnp.where(_same_block(ri, ci, GDN_BASE), low, 0.0)
    inv = eye - base
    power = base
    span = 2
    while span < GDN_BASE:
        power = _dot_f32(power, power)
        inv = _dot_f32(inv, eye + power)
        span *= 2
    size = 2 * GDN_BASE
    while size <= GDN_CHUNK:
        half = size // 2
        off = jnp.where(_same_block(ri, ci, size) & ~_same_block(ri, ci, half), low, 0.0)
        inv = inv - _dot_f32(_dot_f32(inv, off), inv)
        size *= 2
    return inv


def _gdn_kernel(alog_ref, dtb_ref, q_ref, k_ref, v_ref, gate_ref, cwq_ref, cwk_ref, cwv_ref,
                ab_ref, og_ref, o_ref, state_ref, tail_ref, *, heads_per_step, head_dim):
    hblk = pl.program_id(1)
    tb = q_ref.shape[0]
    n_chunks = tb // GDN_CHUNK

    @pl.when(pl.program_id(2) == 0)
    def _():
        state_ref[...] = jnp.zeros_like(state_ref)
        tail_ref[...] = jnp.zeros_like(tail_ref)

    ri = lax.broadcasted_iota(jnp.int32, (tb, tb), 0)
    ci = lax.broadcasted_iota(jnp.int32, (tb, tb), 1)
    same = _same_block(ri, ci, GDN_CHUNK)
    incl = same & (ri >= ci)
    strict = same & (ri > ci)
    cum_lower = incl.astype(F32)
    cum_after = (same & (ri < ci)).astype(F32)

    for hh in range(heads_per_step):
        head = hblk * heads_per_step + hh
        cols = slice(hh * head_dim, (hh + 1) * head_dim)
        q = _conv_silu(q_ref[:, cols], cwq_ref[:, cols], tail_ref.at[0, hh])
        k = _conv_silu(k_ref[:, cols], cwk_ref[:, cols], tail_ref.at[1, hh])
        v = _conv_silu(v_ref[:, cols], cwv_ref[:, cols], tail_ref.at[2, hh])
        q = q * (lax.rsqrt(jnp.sum(q * q, axis=-1, keepdims=True) + EPS) * head_dim ** -0.5)
        k = k * lax.rsqrt(jnp.sum(k * k, axis=-1, keepdims=True) + EPS)

        ab = ab_ref[hh]
        a_log = jnp.zeros((1, tb), F32) + alog_ref[head]
        g_row = -jnp.exp(a_log) * _softplus(ab[0:1] + dtb_ref[head])
        beta_row = jax.nn.sigmoid(ab[1:2])
        g_col = _col_bcast(g_row, head_dim)
        beta = _col_bcast(beta_row, head_dim)
        gc = _dot_f32(cum_lower, g_col)
        g_after = _dot_f32(cum_after, g_col)
        gc_row = gc.T[0:1, :]
        reps = tb // head_dim
        diff = jnp.concatenate([gc] * reps, axis=1) - gc_row
        decay = jnp.where(incl, jnp.exp(jnp.where(incl, diff, 0.0)), 0.0)

        kb = k * beta
        k16 = k.astype(BF16)
        qk = _dot_nt(jnp.concatenate([q, kb], axis=0).astype(BF16), k16)
        attn = (qk[:tb] * decay).astype(BF16)
        low = jnp.where(strict, qk[tb:] * decay, 0.0)
        t_mat = _inv_unit_lower(low, ri, ci)
        e_gc = jnp.exp(gc)
        wu = _dot(t_mat.astype(BF16),
                  jnp.concatenate([kb * e_gc, v * beta], axis=1).astype(BF16))
        w16 = wu[:, :head_dim].astype(BF16)
        u = wu[:, head_dim:]
        qg16 = (q * e_gc).astype(BF16)
        kg16 = (k * jnp.exp(g_after)).astype(BF16)

        state = state_ref[hh]
        outs = []
        for c in range(n_chunks):
            r = slice(c * GDN_CHUNK, (c + 1) * GDN_CHUNK)
            ws_qs = _dot(jnp.concatenate([w16[r], qg16[r]], axis=0), state.astype(BF16))
            v_new = u[r] - ws_qs[:GDN_CHUNK]
            v_new16 = v_new.astype(BF16)
            outs.append(ws_qs[GDN_CHUNK:] + _dot(attn[r, r], v_new16))
            g_last = jnp.exp(gc[(c + 1) * GDN_CHUNK - 1:(c + 1) * GDN_CHUNK, :])
            state = state * g_last + _dot_tn(kg16[r], v_new16)
        state_ref[hh] = state

        o = jnp.concatenate(outs, axis=0)
        gate = gate_ref[:, cols].astype(F32)
        o = o * _rms_scale(o) * og_ref[...] * (gate * jax.nn.sigmoid(gate))
        o_ref[:, cols] = o.astype(o_ref.dtype)


def gdn(proj, conv_w, ab, a_log, dt_bias, out_gain, *, n_heads, head_dim, heads_per_step=1):
    b, t, _ = proj.shape
    tb = min(GDN_BLOCK, t)
    assert t % tb == 0 and tb % GDN_CHUNK == 0 and n_heads % heads_per_step == 0
    assert tb % head_dim == 0 or tb == head_dim
    hb = heads_per_step
    n_hblk = n_heads // hb
    wblk = hb * head_dim

    def act_spec(section):
        return pl.BlockSpec((None, tb, wblk), lambda bi, hi, ti: (bi, ti, section * n_hblk + hi))

    def cw_spec(section):
        return pl.BlockSpec((CONV_TAPS, wblk), lambda bi, hi, ti: (0, section * n_hblk + hi))

    smem = pl.BlockSpec(memory_space=pltpu.SMEM)
    return pl.pallas_call(
        functools.partial(_gdn_kernel, heads_per_step=hb, head_dim=head_dim),
        grid=(b, n_hblk, t // tb),
        in_specs=[smem, smem,
                  act_spec(0), act_spec(1), act_spec(2), act_spec(3),
                  cw_spec(0), cw_spec(1), cw_spec(2),
                  pl.BlockSpec((None, hb, 2, tb), lambda bi, hi, ti: (bi, hi, 0, ti)),
                  pl.BlockSpec((1, head_dim), lambda bi, hi, ti: (0, 0))],
        out_specs=pl.BlockSpec((None, tb, wblk), lambda bi, hi, ti: (bi, ti, hi)),
        out_shape=jax.ShapeDtypeStruct((b, t, n_heads * head_dim), BF16),
        scratch_shapes=[pltpu.VMEM((hb, head_dim, head_dim), F32),
                        pltpu.VMEM((3, hb, V7X_SUBLANES, head_dim), F32)],
        compiler_params=pltpu.CompilerParams(
            dimension_semantics=("arbitrary", "arbitrary", "arbitrary"),
            vmem_limit_bytes=_vmem_limit(32 * 1024 * 1024)),
        name="gdn",
    )(a_log.astype(F32), dt_bias.astype(F32), proj, proj, proj, proj,
      conv_w, conv_w, conv_w, ab, out_gain.reshape(1, head_dim).astype(F32))


def _sb_block(q, kj, vj, upper, csum, mask):
    z = _dot_nt(q, kj)
    sp = jnp.log2(1.0 + jnp.exp2(-jnp.abs(z)))
    log_beta = jnp.minimum(z, 0.0) - sp
    log_1m = log_beta - z
    if mask is not None:
        log_1m = jnp.where(mask, log_1m, 0.0)
    tail = _dot(log_1m.astype(BF16), upper) + csum
    a = jnp.exp2(log_beta + tail)
    if mask is not None:
        a = jnp.where(mask, a, 0.0)
    return _dot(a.astype(BF16), vj), jnp.sum(log_1m, axis=-1, keepdims=True)


def _sb_kernel(q_ref, k_ref, v_ref, o_ref, acc_ref, csum_ref, *, block):
    i = pl.program_id(2)
    q = q_ref[...]
    ri = lax.broadcasted_iota(jnp.int32, (block, block), 0)
    ci = lax.broadcasted_iota(jnp.int32, (block, block), 1)
    upper = (ri > ci).astype(BF16)
    before = ci < ri

    start = pl.multiple_of(i * block, block)
    pv, rs = _sb_block(q, k_ref[pl.ds(start, block), :], v_ref[pl.ds(start, block), :],
                       upper, jnp.zeros((block, 1), F32), before)
    acc_ref[...] = pv
    csum_ref[...] = jnp.broadcast_to(rs, csum_ref.shape)

    def body(jj, carry):
        off = pl.multiple_of((i - 1 - jj) * block, block)
        csum = csum_ref[...]
        pv, rs = _sb_block(q, k_ref[pl.ds(off, block), :], v_ref[pl.ds(off, block), :],
                           upper, csum[:, 0:1], None)
        acc_ref[...] += pv
        csum_ref[...] = csum + rs
        return carry

    lax.fori_loop(0, i, body, 0)
    o_ref[...] = acc_ref[...].astype(o_ref.dtype)


def sb_attn(q, kv, *, n_heads, head_dim, block=256):
    b, t, _ = q.shape
    blk = min(block, t)
    assert t % blk == 0
    return pl.pallas_call(
        functools.partial(_sb_kernel, block=blk),
        grid=(b, n_heads, t // blk),
        in_specs=[pl.BlockSpec((None, blk, head_dim), lambda bi, hi, qi: (bi, qi, hi)),
                  pl.BlockSpec((None, t, head_dim), lambda bi, hi, qi: (bi, 0, hi)),
                  pl.BlockSpec((None, t, head_dim), lambda bi, hi, qi: (bi, 0, n_heads + hi))],
        out_specs=pl.BlockSpec((None, blk, head_dim), lambda bi, hi, qi: (bi, qi, hi)),
        out_shape=jax.ShapeDtypeStruct((b, t, n_heads * head_dim), BF16),
        scratch_shapes=[pltpu.VMEM((blk, head_dim), F32),
                        pltpu.VMEM((blk, V7X_LANES), F32)],
        compiler_params=pltpu.CompilerParams(
            dimension_semantics=("arbitrary", "arbitrary", "arbitrary"),
            vmem_limit_bytes=_vmem_limit(32 * 1024 * 1024)),
        name="sb_attn",
    )(q, kv, kv)


def kernel(x, mix_pre_gain, mix_post_gain, mlp_pre_gain, mlp_post_gain, mlp_w_up, mlp_w_down,
           gdn_w_in, gdn_conv_w, gdn_a_log, gdn_dt_bias, gdn_out_gain, gdn_w_out,
           kv_gain, w_kv, sb_w_q, sb_w_o):
    b, t, d = x.shape
    n = b * t
    n_a = gdn_w_in.shape[0]
    depth = mix_pre_gain.shape[0]
    gdn_heads = gdn_a_log.shape[1]
    gdn_dim = gdn_out_gain.shape[1]
    gdn_width = gdn_heads * gdn_dim
    sb_width = sb_w_q.shape[2]
    sb_dim = gdn_dim
    sb_heads = sb_width // sb_dim
    assert gdn_w_in.shape[2] == 4 * gdn_width + 2 * gdn_heads

    xs = x.reshape(n, d)
    kv = None
    for layer in range(depth):
        if layer < n_a:
            w_in = gdn_w_in[layer]
            w_main = w_in[:, :4 * gdn_width].astype(BF16)
            w_small = jnp.pad(w_in[:, 4 * gdn_width:],
                              ((0, 0), (0, V7X_LANES - 2 * gdn_heads))).astype(BF16)
            pre = mix_pre_gain[layer]
            proj, small = norm_proj(xs, [(pre, w_main, BF16, 1.0), (pre, w_small, F32, 1.0)])
            small = small[:, :2 * gdn_heads].reshape(b, t, 2, gdn_heads)
            ab = jnp.transpose(small[:, :, ::-1, :], (0, 3, 2, 1))
            o = gdn(proj.reshape(b, t, 4 * gdn_width), gdn_conv_w[layer], ab,
                    gdn_a_log[layer], gdn_dt_bias[layer], gdn_out_gain[layer],
                    n_heads=gdn_heads, head_dim=gdn_dim)
            xs = proj_post(o.reshape(n, gdn_width), gdn_w_out[layer].astype(BF16),
                           mix_post_gain[layer], xs)
        else:
            bl = layer - n_a
            q_scale = sb_dim ** -0.5 * math.log2(math.e)
            groups = [(mix_pre_gain[layer], sb_w_q[bl].astype(BF16), BF16, q_scale)]
            if kv is None:
                groups.append((kv_gain, w_kv.astype(BF16), BF16, 1.0))
                q, kv = norm_proj(xs, groups)
                kv = kv.reshape(b, t, 2 * sb_width)
            else:
                (q,) = norm_proj(xs, groups)
            o = sb_attn(q.reshape(b, t, sb_width), kv, n_heads=sb_heads, head_dim=sb_dim)
            xs = proj_post(o.reshape(n, sb_width), sb_w_o[bl].astype(BF16),
                           mix_post_gain[layer], xs)
        xs = mlp(xs, mlp_pre_gain[layer], mlp_w_up[layer].astype(BF16),
                 mlp_w_down[layer].astype(BF16), mlp_post_gain[layer])
    return xs.reshape(b, t, d)
```

```python
import functools
import math

import jax
import jax.numpy as jnp
from jax import lax
from jax.experimental import pallas as pl
from jax.experimental.pallas import tpu as pltpu

F32 = jnp.float32
BF16 = jnp.bfloat16
EPS = 1e-6

V7X_LANES = 128
V7X_SUBLANES = 8
V7X_VMEM_BYTES = 64 * 1024 * 1024
V7X_VMEM_REQUEST_CAP = 56 * 1024 * 1024

GDN_CHUNK = 64
GDN_BLOCK = 256
GDN_BASE = 8
CONV_TAPS = 4


def _vmem_limit(nbytes):
    return int(min(max(nbytes, 16 * 1024 * 1024), V7X_VMEM_REQUEST_CAP))


def _resident(shape, index_map):
    return pl.BlockSpec(shape, index_map, pipeline_mode=pl.Buffered(1))


def _rms_scale(x):
    return lax.rsqrt(jnp.mean(x * x, axis=-1, keepdims=True) + EPS)


def _dot(a, b):
    return jnp.dot(a, b, preferred_element_type=F32)


def _dot_f32(a, b):
    return jnp.dot(a, b, preferred_element_type=F32, precision=lax.Precision.HIGHEST)


def _dot_nt(a, b):
    return lax.dot_general(a, b, (((1,), (1,)), ((), ())), preferred_element_type=F32)


def _dot_tn(a, b):
    return lax.dot_general(a, b, (((0,), (0,)), ((), ())), preferred_element_type=F32)


def _norm_proj_kernel(*refs, n_groups, scales, col_chunk):
    x_ref = refs[0]
    gain_refs = refs[1:1 + n_groups]
    w_refs = refs[1 + n_groups:1 + 2 * n_groups]
    out_refs = refs[1 + 2 * n_groups:]
    x = x_ref[...]
    xn = x * _rms_scale(x)
    for g in range(n_groups):
        h = (xn * gain_refs[g][...]).astype(BF16)
        n_cols = w_refs[g].shape[1]
        step = min(col_chunk, n_cols)
        for c in range(0, n_cols, step):
            acc = _dot(h, w_refs[g][:, c:c + step])
            if scales[g] != 1.0:
                acc = acc * scales[g]
            out_refs[g][:, c:c + step] = acc.astype(out_refs[g].dtype)


def norm_proj(x, groups, *, row_tile=512, col_chunk=1024):
    n, d = x.shape
    tm = min(row_tile, n)
    assert n % tm == 0
    n_groups = len(groups)
    gains = [g[0].reshape(1, d).astype(F32) for g in groups]
    ws = [g[1] for g in groups]
    in_specs = [pl.BlockSpec((tm, d), lambda i: (i, 0))]
    in_specs += [_resident((1, d), lambda i: (0, 0)) for _ in groups]
    in_specs += [_resident(w.shape, lambda i: (0, 0)) for w in ws]
    out_specs = [pl.BlockSpec((tm, w.shape[1]), lambda i: (i, 0)) for w in ws]
    out_shape = [jax.ShapeDtypeStruct((n, w.shape[1]), g[2]) for w, g in zip(ws, groups)]
    vmem = 2 * tm * d * 4 + sum(w.size * 2 for w in ws)
    vmem += sum(2 * tm * w.shape[1] * jnp.dtype(g[2]).itemsize for w, g in zip(ws, groups))
    vmem += 4 * tm * max(d, col_chunk) * 4
    return pl.pallas_call(
        functools.partial(_norm_proj_kernel, n_groups=n_groups,
                          scales=tuple(float(g[3]) for g in groups), col_chunk=col_chunk),
        grid=(n // tm,),
        in_specs=in_specs,
        out_specs=out_specs,
        out_shape=out_shape,
        compiler_params=pltpu.CompilerParams(
            dimension_semantics=("arbitrary",), vmem_limit_bytes=_vmem_limit(vmem)),
        name="norm_proj",
    )(x, *gains, *ws)


def _proj_post_kernel(a_ref, w_ref, gain_ref, r_ref, o_ref):
    y = _dot(a_ref[...], w_ref[...])
    o_ref[...] = r_ref[...] + y * _rms_scale(y) * gain_ref[...]


def proj_post(a, w, gain, resid, *, row_tile=512):
    n, k = a.shape
    d = w.shape[1]
    tm = min(row_tile, n)
    assert n % tm == 0
    vmem = 2 * tm * k * 2 + k * d * 2 + 4 * tm * d * 4 + 2 * tm * d * 4
    return pl.pallas_call(
        _proj_post_kernel,
        grid=(n // tm,),
        in_specs=[pl.BlockSpec((tm, k), lambda i: (i, 0)),
                  _resident((k, d), lambda i: (0, 0)),
                  _resident((1, d), lambda i: (0, 0)),
                  pl.BlockSpec((tm, d), lambda i: (i, 0))],
        out_specs=pl.BlockSpec((tm, d), lambda i: (i, 0)),
        out_shape=jax.ShapeDtypeStruct((n, d), F32),
        compiler_params=pltpu.CompilerParams(
            dimension_semantics=("arbitrary",), vmem_limit_bytes=_vmem_limit(vmem)),
        name="proj_post",
    )(a, w, gain.reshape(1, d).astype(F32), resid)


def _mlp_kernel(x_ref, gpre_ref, wup_ref, wdn_ref, gpost_ref, o_ref, *, ff_chunk):
    x = x_ref[...]
    h = (x * _rms_scale(x) * gpre_ref[...]).astype(BF16)
    d_ff = wup_ref.shape[1]
    y = None
    for c in range(0, d_ff, ff_chunk):
        u = jnp.maximum(_dot(h, wup_ref[:, c:c + ff_chunk]), 0.0)
        part = _dot((u * u).astype(BF16), wdn_ref[c:c + ff_chunk, :])
        y = part if y is None else y + part
    o_ref[...] = x + y * _rms_scale(y) * gpost_ref[...]


def mlp(x, g_pre, w_up, w_down, g_post, *, row_tile=512, ff_chunk=1024):
    n, d = x.shape
    d_ff = w_up.shape[1]
    tm = min(row_tile, n)
    fc = min(ff_chunk, d_ff)
    assert n % tm == 0 and d_ff % fc == 0
    vmem = 4 * tm * d * 4 + 2 * d * d_ff * 2 + 3 * tm * fc * 4 + 2 * tm * d * 4
    return pl.pallas_call(
        functools.partial(_mlp_kernel, ff_chunk=fc),
        grid=(n // tm,),
        in_specs=[pl.BlockSpec((tm, d), lambda i: (i, 0)),
                  _resident((1, d), lambda i: (0, 0)),
                  _resident((d, d_ff), lambda i: (0, 0)),
                  _resident((d_ff, d), lambda i: (0, 0)),
                  _resident((1, d), lambda i: (0, 0))],
        out_specs=pl.BlockSpec((tm, d), lambda i: (i, 0)),
        out_shape=jax.ShapeDtypeStruct((n, d), F32),
        compiler_params=pltpu.CompilerParams(
            dimension_semantics=("arbitrary",), vmem_limit_bytes=_vmem_limit(vmem)),
        name="mlp",
    )(x, g_pre.reshape(1, d).astype(F32), w_up, w_down, g_post.reshape(1, d).astype(F32))


def _shift_rows(x, tail, s):
    xs = pltpu.roll(x, s, axis=0)
    ts = pltpu.roll(tail, s, axis=0)
    row = lax.broadcasted_iota(jnp.int32, tail.shape, 0)
    top = jnp.where(row < s, ts, xs[:V7X_SUBLANES])
    return jnp.concatenate([top, xs[V7X_SUBLANES:]], axis=0)


def _conv_silu(x_bf16, w, tail_ref):
    x = x_bf16.astype(F32)
    tail = tail_ref[...]
    acc = x * w[CONV_TAPS - 1:CONV_TAPS, :]
    for s in range(1, CONV_TAPS):
        acc = acc + _shift_rows(x, tail, s) * w[CONV_TAPS - 1 - s:CONV_TAPS - s, :]
    tail_ref[...] = x[x.shape[0] - V7X_SUBLANES:, :]
    return acc * jax.nn.sigmoid(acc)


def _softplus(x):
    return jnp.maximum(x, 0.0) + jnp.log(1.0 + jnp.exp(-jnp.abs(x)))


def _col_bcast(row, width):
    return jnp.broadcast_to(row, (width, row.shape[1])).T


def _same_block(ri, ci, size):
    shift = size.bit_length() - 1
    assert 1 << shift == size
    return (ri >> shift) == (ci >> shift)


def _inv_unit_lower(low, ri, ci):
    eye = (ri == ci).astype(F32)
    base = jnp.where(_same_block(ri, ci, GDN_BASE), low, 0.0)
    inv = eye - base
    power = base
    span = 2
    while span < GDN_BASE:
        power = _dot_f32(power, power)
        inv = _dot_f32(inv, eye + power)
        span *= 2
    size = 2 * GDN_BASE
    while size <= GDN_CHUNK:
        half = size // 2
        off = jnp.where(_same_block(ri, ci, size) & ~_same_block(ri, ci, half), low, 0.0)
        inv = inv - _dot_f32(_dot_f32(inv, off), inv)
        size *= 2
    return inv


def _gdn_kernel(alog_ref, dtb_ref, q_ref, k_ref, v_ref, gate_ref, cwq_ref, cwk_ref, cwv_ref,
                ab_ref, og_ref, o_ref, state_ref, tail_ref, *, heads_per_step, head_dim):
    hblk = pl.program_id(1)
    tb = q_ref.shape[0]
    n_chunks = tb // GDN_CHUNK

    @pl.when(pl.program_id(2) == 0)
    def _():
        state_ref[...] = jnp.zeros_like(state_ref)
        tail_ref[...] = jnp.zeros_like(tail_ref)

    ri = lax.broadcasted_iota(jnp.int32, (tb, tb), 0)
    ci = lax.broadcasted_iota(jnp.int32, (tb, tb), 1)
    same = _same_block(ri, ci, GDN_CHUNK)
    incl = same & (ri >= ci)
    strict = same & (ri > ci)
    cum_lower = incl.astype(F32)
    cum_after = (same & (ri < ci)).astype(F32)

    for hh in range(heads_per_step):
        head = hblk * heads_per_step + hh
        cols = slice(hh * head_dim, (hh + 1) * head_dim)
        q = _conv_silu(q_ref[:, cols], cwq_ref[:, cols], tail_ref.at[0, hh])
        k = _conv_silu(k_ref[:, cols], cwk_ref[:, cols], tail_ref.at[1, hh])
        v = _conv_silu(v_ref[:, cols], cwv_ref[:, cols], tail_ref.at[2, hh])
        q = q * (lax.rsqrt(jnp.sum(q * q, axis=-1, keepdims=True) + EPS) * head_dim ** -0.5)
        k = k * lax.rsqrt(jnp.sum(k * k, axis=-1, keepdims=True) + EPS)

        ab = ab_ref[hh]
        a_log = jnp.zeros((1, tb), F32) + alog_ref[head]
        g_row = -jnp.exp(a_log) * _softplus(ab[0:1] + dtb_ref[head])
        beta_row = jax.nn.sigmoid(ab[1:2])
        g_col = _col_bcast(g_row, head_dim)
        beta = _col_bcast(beta_row, head_dim)
        gc = _dot_f32(cum_lower, g_col)
        g_after = _dot_f32(cum_after, g_col)
        gc_row = gc.T[0:1, :]
        reps = tb // head_dim
        diff = jnp.concatenate([gc] * reps, axis=1) - gc_row
        decay = jnp.where(incl, jnp.exp(jnp.where(incl, diff, 0.0)), 0.0)

        kb = k * beta
        k16 = k.astype(BF16)
        qk = _dot_nt(jnp.concatenate([q, kb], axis=0).astype(BF16), k16)
        attn = (qk[:tb] * decay).astype(BF16)
        low = jnp.where(strict, qk[tb:] * decay, 0.0)
        t_mat = _inv_unit_lower(low, ri, ci)
        e_gc = jnp.exp(gc)
        wu = _dot(t_mat.astype(BF16),
                  jnp.concatenate([kb * e_gc, v * beta], axis=1).astype(BF16))
        w16 = wu[:, :head_dim].astype(BF16)
        u = wu[:, head_dim:]
        qg16 = (q * e_gc).astype(BF16)
        kg16 = (k * jnp.exp(g_after)).astype(BF16)

        state = state_ref[hh]
        outs = []
        for c in range(n_chunks):
            r = slice(c * GDN_CHUNK, (c + 1) * GDN_CHUNK)
            ws_qs = _dot(jnp.concatenate([w16[r], qg16[r]], axis=0), state.astype(BF16))
            v_new = u[r] - ws_qs[:GDN_CHUNK]
            v_new16 = v_new.astype(BF16)
            outs.append(ws_qs[GDN_CHUNK:] + _dot(attn[r, r], v_new16))
            g_last = jnp.exp(gc[(c + 1) * GDN_CHUNK - 1:(c + 1) * GDN_CHUNK, :])
            state = state * g_last + _dot_tn(kg16[r], v_new16)
        state_ref[hh] = state

        o = jnp.concatenate(outs, axis=0)
        gate = gate_ref[:, cols].astype(F32)
        o = o * _rms_scale(o) * og_ref[...] * (gate * jax.nn.sigmoid(gate))
        o_ref[:, cols] = o.astype(o_ref.dtype)


def gdn(proj, conv_w, ab, a_log, dt_bias, out_gain, *, n_heads, head_dim, heads_per_step=1):
    b, t, _ = proj.shape
    tb = min(GDN_BLOCK, t)
    assert t % tb == 0 and tb % GDN_CHUNK == 0 and n_heads % heads_per_step == 0
    assert tb % head_dim == 0 or tb == head_dim
    hb = heads_per_step
    n_hblk = n_heads // hb
    wblk = hb * head_dim

    def act_spec(section):
        return pl.BlockSpec((None, tb, wblk), lambda bi, hi, ti: (bi, ti, section * n_hblk + hi))

    def cw_spec(section):
        return pl.BlockSpec((CONV_TAPS, wblk), lambda bi, hi, ti: (0, section * n_hblk + hi))

    smem = pl.BlockSpec(memory_space=pltpu.SMEM)
    return pl.pallas_call(
        functools.partial(_gdn_kernel, heads_per_step=hb, head_dim=head_dim),
        grid=(b, n_hblk, t // tb),
        in_specs=[smem, smem,
                  act_spec(0), act_spec(1), act_spec(2), act_spec(3),
                  cw_spec(0), cw_spec(1), cw_spec(2),
                  pl.BlockSpec((None, hb, 2, tb), lambda bi, hi, ti: (bi, hi, 0, ti)),
                  pl.BlockSpec((1, head_dim), lambda bi, hi, ti: (0, 0))],
        out_specs=pl.BlockSpec((None, tb, wblk), lambda bi, hi, ti: (bi, ti, hi)),
        out_shape=jax.ShapeDtypeStruct((b, t, n_heads * head_dim), BF16),
        scratch_shapes=[pltpu.VMEM((hb, head_dim, head_dim), F32),
                        pltpu.VMEM((3, hb, V7X_SUBLANES, head_dim), F32)],
        compiler_params=pltpu.CompilerParams(
            dimension_semantics=("arbitrary", "arbitrary", "arbitrary"),
            vmem_limit_bytes=_vmem_limit(32 * 1024 * 1024)),
        name="gdn",
    )(a_log.astype(F32), dt_bias.astype(F32), proj, proj, proj, proj,
      conv_w, conv_w, conv_w, ab, out_gain.reshape(1, head_dim).astype(F32))


def _sb_logits_stage(q, kj, mask, l1m_ref, lb_ref, rs_ref):
    z = _dot_nt(q, kj)
    sp = jnp.log2(1.0 + jnp.exp2(-jnp.abs(z)))
    log_beta = jnp.minimum(z, 0.0) - sp
    log_1m = log_beta - z
    if mask is not None:
        log_1m = jnp.where(mask, log_1m, 0.0)
        log_beta = jnp.where(mask, log_beta, -jnp.inf)
    l1m_ref[...] = log_1m.astype(BF16)
    lb_ref[...] = log_beta
    rs_ref[...] = jnp.broadcast_to(jnp.sum(log_1m, axis=-1, keepdims=True), rs_ref.shape)


def _sb_value_stage(vj, upper, l1m_ref, lb_ref, rs_ref, csum_ref, acc_ref):
    csum = csum_ref[...]
    reps = lb_ref.shape[1] // csum.shape[1]
    tail = _dot(l1m_ref[...], upper) + jnp.concatenate([csum] * reps, axis=1)
    a = jnp.exp2(lb_ref[...] + tail)
    acc_ref[...] += _dot(a.astype(BF16), vj)
    csum_ref[...] = csum + rs_ref[...]


def _sb_kernel(q_ref, k_ref, v_ref, o_ref, acc_ref, csum_ref, l1m_ref, lb_ref, rs_ref,
               *, block, heads_per_step, head_dim):
    i = pl.program_id(2)
    ri = lax.broadcasted_iota(jnp.int32, (block, block), 0)
    ci = lax.broadcasted_iota(jnp.int32, (block, block), 1)
    upper = (ri > ci).astype(BF16)
    before = ci < ri
    heads = [slice(hh * head_dim, (hh + 1) * head_dim) for hh in range(heads_per_step)]

    def stage_a(hh, off, mask):
        _sb_logits_stage(q_ref[:, heads[hh]], k_ref[pl.ds(off, block), heads[hh]], mask,
                         l1m_ref.at[hh], lb_ref.at[hh], rs_ref.at[hh])

    def stage_b(hh, off):
        _sb_value_stage(v_ref[pl.ds(off, block), heads[hh]], upper, l1m_ref.at[hh],
                        lb_ref.at[hh], rs_ref.at[hh], csum_ref.at[hh], acc_ref.at[hh])

    acc_ref[...] = jnp.zeros_like(acc_ref)
    csum_ref[...] = jnp.zeros_like(csum_ref)
    for hh in range(heads_per_step):
        stage_a(hh, pl.multiple_of(i * block, block), before)

    def body(jj, carry):
        cur = pl.multiple_of((i - jj) * block, block)
        nxt = pl.multiple_of((i - 1 - jj) * block, block)
        for hh in range(heads_per_step):
            stage_b(hh, cur)
            stage_a(hh, nxt, None)
        return carry

    lax.fori_loop(0, i, body, 0)
    for hh in range(heads_per_step):
        stage_b(hh, 0)
        o_ref[:, heads[hh]] = acc_ref[hh].astype(o_ref.dtype)


def sb_attn(q, kv, *, n_heads, head_dim, block=256, heads_per_step=2):
    b, t, _ = q.shape
    blk = min(block, t)
    hp = heads_per_step
    assert t % blk == 0 and n_heads % hp == 0 and head_dim == V7X_LANES
    n_hblk = n_heads // hp
    wblk = hp * head_dim
    vmem = 2 * 2 * t * wblk * 2 + 4 * blk * wblk * 2 + 2 * hp * blk * head_dim * 4
    vmem += 12 * hp * blk * blk * 4
    return pl.pallas_call(
        functools.partial(_sb_kernel, block=blk, heads_per_step=hp, head_dim=head_dim),
        grid=(b, n_hblk, t // blk),
        in_specs=[pl.BlockSpec((None, blk, wblk), lambda bi, hi, qi: (bi, qi, hi)),
                  pl.BlockSpec((None, t, wblk), lambda bi, hi, qi: (bi, 0, hi)),
                  pl.BlockSpec((None, t, wblk), lambda bi, hi, qi: (bi, 0, n_hblk + hi))],
        out_specs=pl.BlockSpec((None, blk, wblk), lambda bi, hi, qi: (bi, qi, hi)),
        out_shape=jax.ShapeDtypeStruct((b, t, n_heads * head_dim), BF16),
        scratch_shapes=[pltpu.VMEM((hp, blk, head_dim), F32),
                        pltpu.VMEM((hp, blk, V7X_LANES), F32),
                        pltpu.VMEM((hp, blk, blk), BF16),
                        pltpu.VMEM((hp, blk, blk), F32),
                        pltpu.VMEM((hp, blk, V7X_LANES), F32)],
        compiler_params=pltpu.CompilerParams(
            dimension_semantics=("arbitrary", "arbitrary", "arbitrary"),
            vmem_limit_bytes=_vmem_limit(vmem)),
        name="sb_attn",
    )(q, kv, kv)


def kernel(x, mix_pre_gain, mix_post_gain, mlp_pre_gain, mlp_post_gain, mlp_w_up, mlp_w_down,
           gdn_w_in, gdn_conv_w, gdn_a_log, gdn_dt_bias, gdn_out_gain, gdn_w_out,
           kv_gain, w_kv, sb_w_q, sb_w_o):
    b, t, d = x.shape
    n = b * t
    n_a = gdn_w_in.shape[0]
    depth = mix_pre_gain.shape[0]
    gdn_heads = gdn_a_log.shape[1]
    gdn_dim = gdn_out_gain.shape[1]
    gdn_width = gdn_heads * gdn_dim
    sb_width = sb_w_q.shape[2]
    sb_dim = gdn_dim
    sb_heads = sb_width // sb_dim
    assert gdn_w_in.shape[2] == 4 * gdn_width + 2 * gdn_heads

    xs = x.reshape(n, d)
    kv = None
    for layer in range(depth):
        if layer < n_a:
            w_in = gdn_w_in[layer]
            w_main = w_in[:, :4 * gdn_width].astype(BF16)
            w_small = jnp.pad(w_in[:, 4 * gdn_width:],
                              ((0, 0), (0, V7X_LANES - 2 * gdn_heads))).astype(BF16)
            pre = mix_pre_gain[layer]
            proj, small = norm_proj(xs, [(pre, w_main, BF16, 1.0), (pre, w_small, F32, 1.0)])
            small = small[:, :2 * gdn_heads].reshape(b, t, 2, gdn_heads)
            ab = jnp.transpose(small[:, :, ::-1, :], (0, 3, 2, 1))
            o = gdn(proj.reshape(b, t, 4 * gdn_width), gdn_conv_w[layer], ab,
                    gdn_a_log[layer], gdn_dt_bias[layer], gdn_out_gain[layer],
                    n_heads=gdn_heads, head_dim=gdn_dim)
            xs = proj_post(o.reshape(n, gdn_width), gdn_w_out[layer].astype(BF16),
                           mix_post_gain[layer], xs)
        else:
            bl = layer - n_a
            q_scale = sb_dim ** -0.5 * math.log2(math.e)
            groups = [(mix_pre_gain[layer], sb_w_q[bl].astype(BF16), BF16, q_scale)]
            if kv is None:
                groups.append((kv_gain, w_kv.astype(BF16), BF16, 1.0))
                q, kv = norm_proj(xs, groups)
                kv = kv.reshape(b, t, 2 * sb_width)
            else:
                (q,) = norm_proj(xs, groups)
            o = sb_attn(q.reshape(b, t, sb_width), kv, n_heads=sb_heads, head_dim=sb_dim)
            xs = proj_post(o.reshape(n, sb_width), sb_w_o[bl].astype(BF16),
                           mix_post_gain[layer], xs)
        xs = mlp(xs, mlp_pre_gain[layer], mlp_w_up[layer].astype(BF16),
                 mlp_w_down[layer].astype(BF16), mlp_post_gain[layer])
    return xs.reshape(b, t, d)
```

```python
import functools
import math

import jax
import jax.numpy as jnp
import numpy as np
from jax import lax
from jax.experimental import pallas as pl
from jax.experimental.pallas import tpu as pltpu

F32 = jnp.float32
BF16 = jnp.bfloat16
EPS = 1e-6

V7X_LANES = 128
V7X_SUBLANES = 8
V7X_VMEM_BYTES = 64 * 1024 * 1024
V7X_VMEM_REQUEST_CAP = 56 * 1024 * 1024

GDN_BLOCK = 256
GDN_BASE = 8
CONV_TAPS = 4


def _vmem_limit(nbytes):
    return int(min(max(nbytes, 16 * 1024 * 1024), V7X_VMEM_REQUEST_CAP))


def _resident(shape, index_map):
    return pl.BlockSpec(shape, index_map, pipeline_mode=pl.Buffered(1))


def _rms_scale(x):
    return lax.rsqrt(jnp.mean(x * x, axis=-1, keepdims=True) + EPS)


def _dot(a, b):
    return jnp.dot(a, b, preferred_element_type=F32)


def _dot_nt(a, b):
    return lax.dot_general(a, b, (((1,), (1,)), ((), ())), preferred_element_type=F32)


def _dot_tn(a, b):
    return lax.dot_general(a, b, (((0,), (0,)), ((), ())), preferred_element_type=F32)


def _norm_proj_kernel(*refs, n_groups, scales, col_chunk):
    x_ref = refs[0]
    gain_refs = refs[1:1 + n_groups]
    w_refs = refs[1 + n_groups:1 + 2 * n_groups]
    out_refs = refs[1 + 2 * n_groups:]
    x = x_ref[...]
    xn = x * _rms_scale(x)
    for g in range(n_groups):
        h = (xn * gain_refs[g][...]).astype(BF16)
        n_cols = w_refs[g].shape[1]
        step = min(col_chunk, n_cols)
        for c in range(0, n_cols, step):
            acc = _dot(h, w_refs[g][:, c:c + step])
            if scales[g] != 1.0:
                acc = acc * scales[g]
            out_refs[g][:, c:c + step] = acc.astype(out_refs[g].dtype)


def norm_proj(x, groups, *, row_tile=512, col_chunk=1024):
    n, d = x.shape
    tm = min(row_tile, n)
    assert n % tm == 0
    n_groups = len(groups)
    gains = [g[0].reshape(1, d).astype(F32) for g in groups]
    ws = [g[1] for g in groups]
    in_specs = [pl.BlockSpec((tm, d), lambda i: (i, 0))]
    in_specs += [_resident((1, d), lambda i: (0, 0)) for _ in groups]
    in_specs += [_resident(w.shape, lambda i: (0, 0)) for w in ws]
    out_specs = [pl.BlockSpec((tm, w.shape[1]), lambda i: (i, 0)) for w in ws]
    out_shape = [jax.ShapeDtypeStruct((n, w.shape[1]), g[2]) for w, g in zip(ws, groups)]
    vmem = 2 * tm * d * 4 + sum(w.size * 2 for w in ws)
    vmem += sum(2 * tm * w.shape[1] * jnp.dtype(g[2]).itemsize for w, g in zip(ws, groups))
    vmem += 4 * tm * max(d, col_chunk) * 4
    return pl.pallas_call(
        functools.partial(_norm_proj_kernel, n_groups=n_groups,
                          scales=tuple(float(g[3]) for g in groups), col_chunk=col_chunk),
        grid=(n // tm,),
        in_specs=in_specs,
        out_specs=out_specs,
        out_shape=out_shape,
        compiler_params=pltpu.CompilerParams(
            dimension_semantics=("arbitrary",), vmem_limit_bytes=_vmem_limit(vmem)),
        name="norm_proj",
    )(x, *gains, *ws)


def _proj_post_kernel(a_ref, w_ref, gain_ref, r_ref, o_ref):
    y = _dot(a_ref[...], w_ref[...])
    o_ref[...] = r_ref[...] + y * _rms_scale(y) * gain_ref[...]


def proj_post(a, w, gain, resid, *, row_tile=512):
    n, k = a.shape
    d = w.shape[1]
    tm = min(row_tile, n)
    assert n % tm == 0
    vmem = 2 * tm * k * 2 + k * d * 2 + 4 * tm * d * 4 + 2 * tm * d * 4
    return pl.pallas_call(
        _proj_post_kernel,
        grid=(n // tm,),
        in_specs=[pl.BlockSpec((tm, k), lambda i: (i, 0)),
                  _resident((k, d), lambda i: (0, 0)),
                  _resident((1, d), lambda i: (0, 0)),
                  pl.BlockSpec((tm, d), lambda i: (i, 0))],
        out_specs=pl.BlockSpec((tm, d), lambda i: (i, 0)),
        out_shape=jax.ShapeDtypeStruct((n, d), F32),
        compiler_params=pltpu.CompilerParams(
            dimension_semantics=("arbitrary",), vmem_limit_bytes=_vmem_limit(vmem)),
        name="proj_post",
    )(a, w, gain.reshape(1, d).astype(F32), resid)


def _mlp_kernel(x_ref, gpre_ref, wup_ref, wdn_ref, gpost_ref, o_ref, *, ff_chunk):
    x = x_ref[...]
    h = (x * _rms_scale(x) * gpre_ref[...]).astype(BF16)
    d_ff = wup_ref.shape[1]
    y = None
    for c in range(0, d_ff, ff_chunk):
        u = jnp.maximum(_dot(h, wup_ref[:, c:c + ff_chunk]), 0.0)
        part = _dot((u * u).astype(BF16), wdn_ref[c:c + ff_chunk, :])
        y = part if y is None else y + part
    o_ref[...] = x + y * _rms_scale(y) * gpost_ref[...]


def mlp(x, g_pre, w_up, w_down, g_post, *, row_tile=512, ff_chunk=1024):
    n, d = x.shape
    d_ff = w_up.shape[1]
    tm = min(row_tile, n)
    fc = min(ff_chunk, d_ff)
    assert n % tm == 0 and d_ff % fc == 0
    vmem = 4 * tm * d * 4 + 2 * d * d_ff * 2 + 3 * tm * fc * 4 + 2 * tm * d * 4
    return pl.pallas_call(
        functools.partial(_mlp_kernel, ff_chunk=fc),
        grid=(n // tm,),
        in_specs=[pl.BlockSpec((tm, d), lambda i: (i, 0)),
                  _resident((1, d), lambda i: (0, 0)),
                  _resident((d, d_ff), lambda i: (0, 0)),
                  _resident((d_ff, d), lambda i: (0, 0)),
                  _resident((1, d), lambda i: (0, 0))],
        out_specs=pl.BlockSpec((tm, d), lambda i: (i, 0)),
        out_shape=jax.ShapeDtypeStruct((n, d), F32),
        compiler_params=pltpu.CompilerParams(
            dimension_semantics=("arbitrary",), vmem_limit_bytes=_vmem_limit(vmem)),
        name="mlp",
    )(x, g_pre.reshape(1, d).astype(F32), w_up, w_down, g_post.reshape(1, d).astype(F32))


def _shift_rows(x, tail, s):
    xs = pltpu.roll(x, s, axis=0)
    ts = pltpu.roll(tail, s, axis=0)
    row = lax.broadcasted_iota(jnp.int32, tail.shape, 0)
    top = jnp.where(row < s, ts, xs[:V7X_SUBLANES])
    return jnp.concatenate([top, xs[V7X_SUBLANES:]], axis=0)


def _conv_silu(x_bf16, w, tail_ref):
    x = x_bf16.astype(F32)
    tail = tail_ref[...]
    acc = x * w[CONV_TAPS - 1:CONV_TAPS, :]
    for s in range(1, CONV_TAPS):
        acc = acc + _shift_rows(x, tail, s) * w[CONV_TAPS - 1 - s:CONV_TAPS - s, :]
    tail_ref[...] = x[x.shape[0] - V7X_SUBLANES:, :]
    return acc * jax.nn.sigmoid(acc)


def _softplus(x):
    return jnp.maximum(x, 0.0) + jnp.log(1.0 + jnp.exp(-jnp.abs(x)))


def _col_bcast(row, width):
    return jnp.broadcast_to(row, (width, row.shape[1])).T


MASK_INCL, MASK_EYE, MASK_BASE, MASK_MERGE0 = 0, 1, 2, 3


def _gdn_masks(n):
    ri = np.arange(n)[:, None]
    ci = np.arange(n)[None, :]
    masks = [ri >= ci, ri == ci, (ri // GDN_BASE == ci // GDN_BASE) & (ri > ci)]
    size = 2 * GDN_BASE
    while size <= n:
        masks.append((ri // size == ci // size) & (ri // (size // 2) > ci // (size // 2)))
        size *= 2
    return np.stack(masks).astype(np.float32)


def _inv_unit_lower(lows, masks_ref):
    eye = masks_ref[MASK_EYE]
    powers = [low * masks_ref[MASK_BASE] for low in lows]
    invs = [eye - p for p in powers]
    span = 2
    while span < GDN_BASE:
        powers = [_dot(p, p).astype(BF16) for p in powers]
        invs = [_dot(inv, eye + p).astype(BF16) for inv, p in zip(invs, powers)]
        span *= 2
    for level in range(MASK_MERGE0, masks_ref.shape[0]):
        halves = [_dot(inv, low * masks_ref[level]).astype(BF16) for inv, low in zip(invs, lows)]
        invs = [_dot(eye - h, inv).astype(BF16) for h, inv in zip(halves, invs)]
    return invs


def _gdn_head_inputs(hh, head, alog_ref, dtb_ref, q_ref, k_ref, v_ref, cwq_ref, cwk_ref, cwv_ref,
                     ab_ref, masks_ref, bias_ref, tail_ref, head_dim):
    tb = q_ref.shape[0]
    cols = slice(hh * head_dim, (hh + 1) * head_dim)
    q = _conv_silu(q_ref[:, cols], cwq_ref[:, cols], tail_ref.at[0, hh])
    k = _conv_silu(k_ref[:, cols], cwk_ref[:, cols], tail_ref.at[1, hh])
    v = _conv_silu(v_ref[:, cols], cwv_ref[:, cols], tail_ref.at[2, hh])
    q = q * (lax.rsqrt(jnp.sum(q * q, axis=-1, keepdims=True) + EPS) * head_dim ** -0.5)
    k = k * lax.rsqrt(jnp.sum(k * k, axis=-1, keepdims=True) + EPS)

    ab = ab_ref[hh]
    a_log = jnp.zeros((1, tb), F32) + alog_ref[head]
    g_row = -jnp.exp(a_log) * _softplus(ab[0:1] + dtb_ref[head])
    beta_row = jax.nn.sigmoid(ab[1:2])
    g_col = _col_bcast(g_row, head_dim)
    beta = _col_bcast(beta_row, head_dim)
    tril16 = masks_ref[MASK_INCL]
    g_hi = g_col.astype(BF16)
    g_lo = (g_col - g_hi.astype(F32)).astype(BF16)
    gc = _dot(tril16, g_hi) + _dot(tril16, g_lo)
    gc_row = gc.T[0:1, :]
    gc_last = gc[tb - 1:tb, :]
    diff = jnp.concatenate([gc] * (tb // head_dim), axis=1) - gc_row
    decay = jnp.exp(diff + bias_ref[...])

    kb = k * beta
    qk = _dot_nt(jnp.concatenate([q, kb], axis=0).astype(BF16), k.astype(BF16))
    e_gc = jnp.exp(gc)
    return dict(
        attn16=(qk[:tb] * decay).astype(BF16),
        low16=(qk[tb:] * decay).astype(BF16),
        wu_rhs16=jnp.concatenate([kb * e_gc, v * beta], axis=1).astype(BF16),
        qg16=(q * e_gc).astype(BF16),
        kg16=(k * jnp.exp(gc_last - gc)).astype(BF16),
        g_last=jnp.exp(gc_last))


def _gdn_kernel(alog_ref, dtb_ref, q_ref, k_ref, v_ref, gate_ref, cwq_ref, cwk_ref, cwv_ref,
                ab_ref, og_ref, masks_ref, bias_ref, o_ref, state_ref, tail_ref,
                *, heads_per_step, head_dim):
    hblk = pl.program_id(1)
    tb = q_ref.shape[0]
    heads = range(heads_per_step)

    @pl.when(pl.program_id(2) == 0)
    def _():
        state_ref[...] = jnp.zeros_like(state_ref)
        tail_ref[...] = jnp.zeros_like(tail_ref)

    hd = [_gdn_head_inputs(hh, hblk * heads_per_step + hh, alog_ref, dtb_ref, q_ref, k_ref, v_ref,
                           cwq_ref, cwk_ref, cwv_ref, ab_ref, masks_ref, bias_ref, tail_ref,
                           head_dim)
          for hh in heads]
    t_mats = _inv_unit_lower([h["low16"] for h in hd], masks_ref)
    wus = [_dot(t, h["wu_rhs16"]) for t, h in zip(t_mats, hd)]
    states = [state_ref[hh] for hh in heads]
    ws_qs = [_dot(jnp.concatenate([wu[:, :head_dim].astype(BF16), h["qg16"]], axis=0),
                  s.astype(BF16)) for wu, h, s in zip(wus, hd, states)]
    v_new16 = [(wu[:, head_dim:] - wq[:tb]).astype(BF16) for wu, wq in zip(wus, ws_qs)]
    outs = [wq[tb:] + _dot(h["attn16"], vn) for wq, h, vn in zip(ws_qs, hd, v_new16)]
    for hh in heads:
        state_ref[hh] = states[hh] * hd[hh]["g_last"] + _dot_tn(hd[hh]["kg16"], v_new16[hh])
    for hh in heads:
        cols = slice(hh * head_dim, (hh + 1) * head_dim)
        gate = gate_ref[:, cols].astype(F32)
        o = outs[hh]
        o = o * _rms_scale(o) * og_ref[...] * (gate * jax.nn.sigmoid(gate))
        o_ref[:, cols] = o.astype(o_ref.dtype)


def gdn(proj, conv_w, ab, a_log, dt_bias, out_gain, *, n_heads, head_dim, heads_per_step=4):
    b, t, _ = proj.shape
    tb = min(GDN_BLOCK, t)
    assert t % tb == 0 and tb % head_dim == 0 and n_heads % heads_per_step == 0
    hb = heads_per_step
    n_hblk = n_heads // hb
    wblk = hb * head_dim
    masks_np = _gdn_masks(tb)
    masks = jnp.asarray(masks_np, dtype=BF16)
    bias = jnp.asarray(np.where(masks_np[MASK_INCL] > 0, 0.0, -np.inf), dtype=F32)
    vmem = masks.size * 2 + bias.size * 4 + 2 * 5 * tb * wblk * 2 + 40 * hb * tb * tb * 4

    def act_spec(section):
        return pl.BlockSpec((None, tb, wblk), lambda bi, hi, ti: (bi, ti, section * n_hblk + hi))

    def cw_spec(section):
        return pl.BlockSpec((CONV_TAPS, wblk), lambda bi, hi, ti: (0, section * n_hblk + hi))

    smem = pl.BlockSpec(memory_space=pltpu.SMEM)
    return pl.pallas_call(
        functools.partial(_gdn_kernel, heads_per_step=hb, head_dim=head_dim),
        grid=(b, n_hblk, t // tb),
        in_specs=[smem, smem,
                  act_spec(0), act_spec(1), act_spec(2), act_spec(3),
                  cw_spec(0), cw_spec(1), cw_spec(2),
                  pl.BlockSpec((None, hb, 2, tb), lambda bi, hi, ti: (bi, hi, 0, ti)),
                  pl.BlockSpec((1, head_dim), lambda bi, hi, ti: (0, 0)),
                  _resident(masks.shape, lambda bi, hi, ti: (0, 0, 0)),
                  _resident(bias.shape, lambda bi, hi, ti: (0, 0))],
        out_specs=pl.BlockSpec((None, tb, wblk), lambda bi, hi, ti: (bi, ti, hi)),
        out_shape=jax.ShapeDtypeStruct((b, t, n_heads * head_dim), BF16),
        scratch_shapes=[pltpu.VMEM((hb, head_dim, head_dim), F32),
                        pltpu.VMEM((3, hb, V7X_SUBLANES, head_dim), F32)],
        compiler_params=pltpu.CompilerParams(
            dimension_semantics=("arbitrary", "arbitrary", "arbitrary"),
            vmem_limit_bytes=_vmem_limit(vmem)),
        name="gdn",
    )(a_log.astype(F32), dt_bias.astype(F32), proj, proj, proj, proj,
      conv_w, conv_w, conv_w, ab, out_gain.reshape(1, head_dim).astype(F32), masks, bias)


def _sb_logits_stage(q, kj, mask, l1m_ref, lb_ref, rs_ref):
    z = _dot_nt(q, kj)
    sp = jnp.log2(1.0 + jnp.exp2(-jnp.abs(z)))
    log_beta = jnp.minimum(z, 0.0) - sp
    log_1m = log_beta - z
    if mask is not None:
        log_1m = jnp.where(mask, log_1m, 0.0)
        log_beta = jnp.where(mask, log_beta, -jnp.inf)
    l1m_ref[...] = log_1m.astype(BF16)
    lb_ref[...] = log_beta
    rs_ref[...] = jnp.broadcast_to(jnp.sum(log_1m, axis=-1, keepdims=True), rs_ref.shape)


def _sb_weights_stage(upper, l1m_ref, lb_ref, rs_ref, csum_ref, a_ref):
    csum = csum_ref[...]
    reps = lb_ref.shape[1] // csum.shape[1]
    tail = _dot(l1m_ref[...], upper) + jnp.concatenate([csum] * reps, axis=1)
    a_ref[...] = jnp.exp2(lb_ref[...] + tail).astype(BF16)
    csum_ref[...] = csum + rs_ref[...]


def _sb_kernel(q_ref, k_ref, v_ref, o_ref, acc_ref, csum_ref, l1m_ref, lb_ref, rs_ref, a_ref,
               *, block, heads_per_step, head_dim):
    i = pl.program_id(2)
    ri = lax.broadcasted_iota(jnp.int32, (block, block), 0)
    ci = lax.broadcasted_iota(jnp.int32, (block, block), 1)
    upper = (ri > ci).astype(BF16)
    before = ci < ri
    heads = [slice(hh * head_dim, (hh + 1) * head_dim) for hh in range(heads_per_step)]

    def rows(blk):
        return pl.ds(pl.multiple_of(blk * block, block), block)

    def stage_a(blk, mask):
        for hh, cols in enumerate(heads):
            _sb_logits_stage(q_ref[:, cols], k_ref[rows(blk), cols], mask,
                             l1m_ref.at[hh], lb_ref.at[hh], rs_ref.at[hh])

    def stage_b():
        for hh in range(heads_per_step):
            _sb_weights_stage(upper, l1m_ref.at[hh], lb_ref.at[hh], rs_ref.at[hh],
                              csum_ref.at[hh], a_ref.at[hh])

    def stage_c(blk):
        for hh, cols in enumerate(heads):
            acc_ref[hh] += _dot(a_ref[hh], v_ref[rows(blk), cols])

    acc_ref[...] = jnp.zeros_like(acc_ref)
    csum_ref[...] = jnp.zeros_like(csum_ref)
    stage_a(i, before)

    @pl.when(i >= 1)
    def _():
        stage_b()
        stage_a(i - 1, None)

    def body(t, carry):
        blk = i - t
        stage_c(blk + 2)
        stage_b()
        stage_a(blk, None)
        return carry

    lax.fori_loop(2, i + 1, body, 0)

    @pl.when(i >= 1)
    def _():
        stage_c(1)
        stage_b()

    @pl.when(i == 0)
    def _():
        stage_b()

    stage_c(0)
    for hh, cols in enumerate(heads):
        o_ref[:, cols] = acc_ref[hh].astype(o_ref.dtype)


def sb_attn(q, kv, *, n_heads, head_dim, block=256, heads_per_step=4):
    b, t, _ = q.shape
    blk = min(block, t)
    hp = heads_per_step
    assert t % blk == 0 and n_heads % hp == 0 and head_dim == V7X_LANES
    n_hblk = n_heads // hp
    wblk = hp * head_dim
    vmem = 2 * 2 * t * wblk * 2 + 4 * blk * wblk * 2 + 2 * hp * blk * head_dim * 4
    vmem += 12 * hp * blk * blk * 4
    return pl.pallas_call(
        functools.partial(_sb_kernel, block=blk, heads_per_step=hp, head_dim=head_dim),
        grid=(b, n_hblk, t // blk),
        in_specs=[pl.BlockSpec((None, blk, wblk), lambda bi, hi, qi: (bi, qi, hi)),
                  pl.BlockSpec((None, t, wblk), lambda bi, hi, qi: (bi, 0, hi)),
                  pl.BlockSpec((None, t, wblk), lambda bi, hi, qi: (bi, 0, n_hblk + hi))],
        out_specs=pl.BlockSpec((None, blk, wblk), lambda bi, hi, qi: (bi, qi, hi)),
        out_shape=jax.ShapeDtypeStruct((b, t, n_heads * head_dim), BF16),
        scratch_shapes=[pltpu.VMEM((hp, blk, head_dim), F32),
                        pltpu.VMEM((hp, blk, V7X_LANES), F32),
                        pltpu.VMEM((hp, blk, blk), BF16),
                        pltpu.VMEM((hp, blk, blk), F32),
                        pltpu.VMEM((hp, blk, V7X_LANES), F32),
                        pltpu.VMEM((hp, blk, blk), BF16)],
        compiler_params=pltpu.CompilerParams(
            dimension_semantics=("arbitrary", "arbitrary", "arbitrary"),
            vmem_limit_bytes=_vmem_limit(vmem)),
        name="sb_attn",
    )(q, kv, kv)


def kernel(x, mix_pre_gain, mix_post_gain, mlp_pre_gain, mlp_post_gain, mlp_w_up, mlp_w_down,
           gdn_w_in, gdn_conv_w, gdn_a_log, gdn_dt_bias, gdn_out_gain, gdn_w_out,
           kv_gain, w_kv, sb_w_q, sb_w_o):
    b, t, d = x.shape
    n = b * t
    n_a = gdn_w_in.shape[0]
    depth = mix_pre_gain.shape[0]
    gdn_heads = gdn_a_log.shape[1]
    gdn_dim = gdn_out_gain.shape[1]
    gdn_width = gdn_heads * gdn_dim
    sb_width = sb_w_q.shape[2]
    sb_dim = gdn_dim
    sb_heads = sb_width // sb_dim
    assert gdn_w_in.shape[2] == 4 * gdn_width + 2 * gdn_heads

    xs = x.reshape(n, d)
    kv = None
    for layer in range(depth):
        if layer < n_a:
            w_in = gdn_w_in[layer]
            w_main = w_in[:, :4 * gdn_width].astype(BF16)
            w_small = jnp.pad(w_in[:, 4 * gdn_width:],
                              ((0, 0), (0, V7X_LANES - 2 * gdn_heads))).astype(BF16)
            pre = mix_pre_gain[layer]
            proj, small = norm_proj(xs, [(pre, w_main, BF16, 1.0), (pre, w_small, F32, 1.0)])
            small = small[:, :2 * gdn_heads].reshape(b, t, 2, gdn_heads)
            ab = jnp.transpose(small[:, :, ::-1, :], (0, 3, 2, 1))
            o = gdn(proj.reshape(b, t, 4 * gdn_width), gdn_conv_w[layer], ab,
                    gdn_a_log[layer], gdn_dt_bias[layer], gdn_out_gain[layer],
                    n_heads=gdn_heads, head_dim=gdn_dim)
            xs = proj_post(o.reshape(n, gdn_width), gdn_w_out[layer].astype(BF16),
                           mix_post_gain[layer], xs)
        else:
            bl = layer - n_a
            q_scale = sb_dim ** -0.5 * math.log2(math.e)
            groups = [(mix_pre_gain[layer], sb_w_q[bl].astype(BF16), BF16, q_scale)]
            if kv is None:
                groups.append((kv_gain, w_kv.astype(BF16), BF16, 1.0))
                q, kv = norm_proj(xs, groups)
                kv = kv.reshape(b, t, 2 * sb_width)
            else:
                (q,) = norm_proj(xs, groups)
            o = sb_attn(q.reshape(b, t, sb_width), kv, n_heads=sb_heads, head_dim=sb_dim)
            xs = proj_post(o.reshape(n, sb_width), sb_w_o[bl].astype(BF16),
                           mix_post_gain[layer], xs)
        xs = mlp(xs, mlp_pre_gain[layer], mlp_w_up[layer].astype(BF16),
                 mlp_w_down[layer].astype(BF16), mlp_post_gain[layer])
    return xs.reshape(b, t, d)
```

```python
import functools
import math

import jax
import jax.numpy as jnp
import numpy as np
from jax import lax
from jax.experimental import pallas as pl
from jax.experimental.pallas import tpu as pltpu

F32 = jnp.float32
BF16 = jnp.bfloat16
EPS = 1e-6

V7X_LANES = 128
V7X_SUBLANES = 8
V7X_VMEM_BYTES = 64 * 1024 * 1024
V7X_VMEM_REQUEST_CAP = 56 * 1024 * 1024

GDN_BLOCK = 256
GDN_BASE = 8
CONV_TAPS = 4


def _vmem_limit(nbytes):
    return int(min(max(nbytes, 16 * 1024 * 1024), V7X_VMEM_REQUEST_CAP))


def _resident(shape, index_map):
    return pl.BlockSpec(shape, index_map, pipeline_mode=pl.Buffered(1))


def _rms_scale(x):
    return lax.rsqrt(jnp.mean(x * x, axis=-1, keepdims=True) + EPS)


def _dot(a, b):
    return jnp.dot(a, b, preferred_element_type=F32)


def _dot_nt(a, b):
    return lax.dot_general(a, b, (((1,), (1,)), ((), ())), preferred_element_type=F32)


def _dot_tn(a, b):
    return lax.dot_general(a, b, (((0,), (0,)), ((), ())), preferred_element_type=F32)


def _norm_proj_kernel(*refs, n_groups, scales, col_chunk):
    x_ref = refs[0]
    gain_refs = refs[1:1 + n_groups]
    w_refs = refs[1 + n_groups:1 + 2 * n_groups]
    out_refs = refs[1 + 2 * n_groups:]
    x = x_ref[...]
    xn = x * _rms_scale(x)
    for g in range(n_groups):
        h = (xn * gain_refs[g][...]).astype(BF16)
        n_cols = w_refs[g].shape[1]
        step = min(col_chunk, n_cols)
        for c in range(0, n_cols, step):
            acc = _dot(h, w_refs[g][:, c:c + step])
            if scales[g] != 1.0:
                acc = acc * scales[g]
            out_refs[g][:, c:c + step] = acc.astype(out_refs[g].dtype)


def norm_proj(x, groups, *, row_tile=512, col_chunk=1024):
    n, d = x.shape
    tm = min(row_tile, n)
    assert n % tm == 0
    n_groups = len(groups)
    gains = [g[0].reshape(1, d).astype(F32) for g in groups]
    ws = [g[1] for g in groups]
    in_specs = [pl.BlockSpec((tm, d), lambda i: (i, 0))]
    in_specs += [_resident((1, d), lambda i: (0, 0)) for _ in groups]
    in_specs += [_resident(w.shape, lambda i: (0, 0)) for w in ws]
    out_specs = [pl.BlockSpec((tm, w.shape[1]), lambda i: (i, 0)) for w in ws]
    out_shape = [jax.ShapeDtypeStruct((n, w.shape[1]), g[2]) for w, g in zip(ws, groups)]
    vmem = 2 * tm * d * 4 + sum(w.size * 2 for w in ws)
    vmem += sum(2 * tm * w.shape[1] * jnp.dtype(g[2]).itemsize for w, g in zip(ws, groups))
    vmem += 4 * tm * max(d, col_chunk) * 4
    return pl.pallas_call(
        functools.partial(_norm_proj_kernel, n_groups=n_groups,
                          scales=tuple(float(g[3]) for g in groups), col_chunk=col_chunk),
        grid=(n // tm,),
        in_specs=in_specs,
        out_specs=out_specs,
        out_shape=out_shape,
        compiler_params=pltpu.CompilerParams(
            dimension_semantics=("arbitrary",), vmem_limit_bytes=_vmem_limit(vmem)),
        name="norm_proj",
    )(x, *gains, *ws)


def _post_mlp_kernel(a_ref, wo_ref, gmix_ref, r_ref, gpre_ref, wup_ref, wdn_ref, gpost_ref, o_ref,
                     *, ff_chunk):
    mix = _dot(a_ref[...], wo_ref[...])
    x = r_ref[...] + mix * _rms_scale(mix) * gmix_ref[...]
    h = (x * _rms_scale(x) * gpre_ref[...]).astype(BF16)
    d_ff = wup_ref.shape[1]
    y = None
    for c in range(0, d_ff, ff_chunk):
        u = jnp.maximum(_dot(h, wup_ref[:, c:c + ff_chunk]), 0.0)
        part = _dot((u * u).astype(BF16), wdn_ref[c:c + ff_chunk, :])
        y = part if y is None else y + part
    o_ref[...] = x + y * _rms_scale(y) * gpost_ref[...]


def post_mlp(a, w_o, g_mix, resid, g_pre, w_up, w_down, g_post, *, row_tile=512, ff_chunk=1024):
    n, k = a.shape
    d = w_o.shape[1]
    d_ff = w_up.shape[1]
    tm = min(row_tile, n)
    fc = min(ff_chunk, d_ff)
    assert n % tm == 0 and d_ff % fc == 0

    def row_spec(width):
        return pl.BlockSpec((tm, width), lambda i: (i, 0))

    def gain(g):
        return g.reshape(1, d).astype(F32)

    vmem = 2 * tm * k * 2 + 4 * tm * d * 4 + (k * d + 2 * d * d_ff) * 2
    vmem += 3 * tm * fc * 4 + 4 * tm * d * 4
    return pl.pallas_call(
        functools.partial(_post_mlp_kernel, ff_chunk=fc),
        grid=(n // tm,),
        in_specs=[row_spec(k),
                  _resident((k, d), lambda i: (0, 0)),
                  _resident((1, d), lambda i: (0, 0)),
                  row_spec(d),
                  _resident((1, d), lambda i: (0, 0)),
                  _resident((d, d_ff), lambda i: (0, 0)),
                  _resident((d_ff, d), lambda i: (0, 0)),
                  _resident((1, d), lambda i: (0, 0))],
        out_specs=row_spec(d),
        out_shape=jax.ShapeDtypeStruct((n, d), F32),
        compiler_params=pltpu.CompilerParams(
            dimension_semantics=("arbitrary",), vmem_limit_bytes=_vmem_limit(vmem)),
        name="post_mlp",
    )(a, w_o, gain(g_mix), resid, gain(g_pre), w_up, w_down, gain(g_post))


def _shift_rows(x, tail, s):
    xs = pltpu.roll(x, s, axis=0)
    ts = pltpu.roll(tail, s, axis=0)
    row = lax.broadcasted_iota(jnp.int32, tail.shape, 0)
    top = jnp.where(row < s, ts, xs[:V7X_SUBLANES])
    return jnp.concatenate([top, xs[V7X_SUBLANES:]], axis=0)


def _conv_silu(x_bf16, w, tail_ref):
    x = x_bf16.astype(F32)
    tail = tail_ref[...]
    acc = x * w[CONV_TAPS - 1:CONV_TAPS, :]
    for s in range(1, CONV_TAPS):
        acc = acc + _shift_rows(x, tail, s) * w[CONV_TAPS - 1 - s:CONV_TAPS - s, :]
    tail_ref[...] = x[x.shape[0] - V7X_SUBLANES:, :]
    return acc * jax.nn.sigmoid(acc)


def _softplus(x):
    return jnp.maximum(x, 0.0) + jnp.log(1.0 + jnp.exp(-jnp.abs(x)))


def _col_bcast(row, width):
    return jnp.broadcast_to(row, (width, row.shape[1])).T


MASK_INCL, MASK_EYE, MASK_BASE, MASK_MERGE0 = 0, 1, 2, 3


def _gdn_masks(n):
    ri = np.arange(n)[:, None]
    ci = np.arange(n)[None, :]
    masks = [ri >= ci, ri == ci, (ri // GDN_BASE == ci // GDN_BASE) & (ri > ci)]
    size = 2 * GDN_BASE
    while size <= n:
        masks.append((ri // size == ci // size) & (ri // (size // 2) > ci // (size // 2)))
        size *= 2
    return np.stack(masks).astype(np.float32)


def _inv_unit_lower(lows, masks_ref):
    eye = masks_ref[MASK_EYE]
    powers = [low * masks_ref[MASK_BASE] for low in lows]
    invs = [eye - p for p in powers]
    span = 2
    while span < GDN_BASE:
        powers = [_dot(p, p).astype(BF16) for p in powers]
        invs = [_dot(inv, eye + p).astype(BF16) for inv, p in zip(invs, powers)]
        span *= 2
    for level in range(MASK_MERGE0, masks_ref.shape[0]):
        halves = [_dot(inv, low * masks_ref[level]).astype(BF16) for inv, low in zip(invs, lows)]
        invs = [_dot(eye - h, inv).astype(BF16) for h, inv in zip(halves, invs)]
    return invs


def _gdn_head_inputs(hh, head, alog_ref, dtb_ref, q_ref, k_ref, v_ref, cwq_ref, cwk_ref, cwv_ref,
                     ab_ref, masks_ref, bias_ref, tail_ref, head_dim):
    tb = q_ref.shape[0]
    cols = slice(hh * head_dim, (hh + 1) * head_dim)
    q = _conv_silu(q_ref[:, cols], cwq_ref[:, cols], tail_ref.at[0, hh])
    k = _conv_silu(k_ref[:, cols], cwk_ref[:, cols], tail_ref.at[1, hh])
    v = _conv_silu(v_ref[:, cols], cwv_ref[:, cols], tail_ref.at[2, hh])
    q = q * (lax.rsqrt(jnp.sum(q * q, axis=-1, keepdims=True) + EPS) * head_dim ** -0.5)
    k = k * lax.rsqrt(jnp.sum(k * k, axis=-1, keepdims=True) + EPS)

    ab = ab_ref[hh]
    a_log = jnp.zeros((1, tb), F32) + alog_ref[head]
    g_row = -jnp.exp(a_log) * _softplus(ab[0:1] + dtb_ref[head])
    beta_row = jax.nn.sigmoid(ab[1:2])
    g_col = _col_bcast(g_row, head_dim)
    beta = _col_bcast(beta_row, head_dim)
    tril16 = masks_ref[MASK_INCL]
    g_hi = g_col.astype(BF16)
    g_lo = (g_col - g_hi.astype(F32)).astype(BF16)
    gc = _dot(tril16, g_hi) + _dot(tril16, g_lo)
    gc_row = gc.T[0:1, :]
    gc_last = gc[tb - 1:tb, :]
    diff = jnp.concatenate([gc] * (tb // head_dim), axis=1) - gc_row
    decay = jnp.exp(diff + bias_ref[...])

    kb = k * beta
    qk = _dot_nt(jnp.concatenate([q, kb], axis=0).astype(BF16), k.astype(BF16))
    e_gc = jnp.exp(gc)
    return dict(
        attn16=(qk[:tb] * decay).astype(BF16),
        low16=(qk[tb:] * decay).astype(BF16),
        wu_rhs16=jnp.concatenate([kb * e_gc, v * beta], axis=1).astype(BF16),
        qg16=(q * e_gc).astype(BF16),
        kg16=(k * jnp.exp(gc_last - gc)).astype(BF16),
        g_last=jnp.exp(gc_last))


def _gdn_kernel(alog_ref, dtb_ref, q_ref, k_ref, v_ref, gate_ref, cwq_ref, cwk_ref, cwv_ref,
                ab_ref, og_ref, masks_ref, bias_ref, o_ref, state_ref, tail_ref,
                *, heads_per_step, head_dim):
    hblk = pl.program_id(1)
    tb = q_ref.shape[0]
    heads = range(heads_per_step)

    @pl.when(pl.program_id(2) == 0)
    def _():
        state_ref[...] = jnp.zeros_like(state_ref)
        tail_ref[...] = jnp.zeros_like(tail_ref)

    hd = [_gdn_head_inputs(hh, hblk * heads_per_step + hh, alog_ref, dtb_ref, q_ref, k_ref, v_ref,
                           cwq_ref, cwk_ref, cwv_ref, ab_ref, masks_ref, bias_ref, tail_ref,
                           head_dim)
          for hh in heads]
    t_mats = _inv_unit_lower([h["low16"] for h in hd], masks_ref)
    wus = [_dot(t, h["wu_rhs16"]) for t, h in zip(t_mats, hd)]
    states = [state_ref[hh] for hh in heads]
    ws_qs = [_dot(jnp.concatenate([wu[:, :head_dim].astype(BF16), h["qg16"]], axis=0),
                  s.astype(BF16)) for wu, h, s in zip(wus, hd, states)]
    v_new16 = [(wu[:, head_dim:] - wq[:tb]).astype(BF16) for wu, wq in zip(wus, ws_qs)]
    outs = [wq[tb:] + _dot(h["attn16"], vn) for wq, h, vn in zip(ws_qs, hd, v_new16)]
    for hh in heads:
        state_ref[hh] = states[hh] * hd[hh]["g_last"] + _dot_tn(hd[hh]["kg16"], v_new16[hh])
    for hh in heads:
        cols = slice(hh * head_dim, (hh + 1) * head_dim)
        gate = gate_ref[:, cols].astype(F32)
        o = outs[hh]
        o = o * _rms_scale(o) * og_ref[...] * (gate * jax.nn.sigmoid(gate))
        o_ref[:, cols] = o.astype(o_ref.dtype)


def gdn(proj, conv_w, ab, a_log, dt_bias, out_gain, *, n_heads, head_dim, heads_per_step=4):
    b, t, _ = proj.shape
    tb = min(GDN_BLOCK, t)
    assert t % tb == 0 and tb % head_dim == 0 and n_heads % heads_per_step == 0
    hb = heads_per_step
    n_hblk = n_heads // hb
    wblk = hb * head_dim
    masks_np = _gdn_masks(tb)
    masks = jnp.asarray(masks_np, dtype=BF16)
    bias = jnp.asarray(np.where(masks_np[MASK_INCL] > 0, 0.0, -np.inf), dtype=F32)
    vmem = masks.size * 2 + bias.size * 4 + 2 * 5 * tb * wblk * 2 + 40 * hb * tb * tb * 4

    def act_spec(section):
        return pl.BlockSpec((None, tb, wblk), lambda bi, hi, ti: (bi, ti, section * n_hblk + hi))

    def cw_spec(section):
        return pl.BlockSpec((CONV_TAPS, wblk), lambda bi, hi, ti: (0, section * n_hblk + hi))

    smem = pl.BlockSpec(memory_space=pltpu.SMEM)
    return pl.pallas_call(
        functools.partial(_gdn_kernel, heads_per_step=hb, head_dim=head_dim),
        grid=(b, n_hblk, t // tb),
        in_specs=[smem, smem,
                  act_spec(0), act_spec(1), act_spec(2), act_spec(3),
                  cw_spec(0), cw_spec(1), cw_spec(2),
                  pl.BlockSpec((None, hb, 2, tb), lambda bi, hi, ti: (bi, hi, 0, ti)),
                  pl.BlockSpec((1, head_dim), lambda bi, hi, ti: (0, 0)),
                  _resident(masks.shape, lambda bi, hi, ti: (0, 0, 0)),
                  _resident(bias.shape, lambda bi, hi, ti: (0, 0))],
        out_specs=pl.BlockSpec((None, tb, wblk), lambda bi, hi, ti: (bi, ti, hi)),
        out_shape=jax.ShapeDtypeStruct((b, t, n_heads * head_dim), BF16),
        scratch_shapes=[pltpu.VMEM((hb, head_dim, head_dim), F32),
                        pltpu.VMEM((3, hb, V7X_SUBLANES, head_dim), F32)],
        compiler_params=pltpu.CompilerParams(
            dimension_semantics=("arbitrary", "arbitrary", "arbitrary"),
            vmem_limit_bytes=_vmem_limit(vmem)),
        name="gdn",
    )(a_log.astype(F32), dt_bias.astype(F32), proj, proj, proj, proj,
      conv_w, conv_w, conv_w, ab, out_gain.reshape(1, head_dim).astype(F32), masks, bias)


def _sb_kernel(q_ref, k_ref, v_ref, o_ref, l1m_ref, lb_ref, rs_ref, a_ref, csum_ref, acc_ref,
               *, block, heads_per_step, head_dim):
    i = pl.program_id(2)
    ri = lax.broadcasted_iota(jnp.int32, (block, block), 0)
    ci = lax.broadcasted_iota(jnp.int32, (block, block), 1)
    upper = (ri > ci).astype(BF16)
    before = ci < ri
    heads = [slice(hh * head_dim, (hh + 1) * head_dim) for hh in range(heads_per_step)]

    def rows(blk):
        return pl.ds(pl.multiple_of(blk * block, block), block)

    def stage_a(blk, mask):
        for hh, cols in enumerate(heads):
            z = _dot_nt(q_ref[:, cols], k_ref[rows(blk), cols])
            sp = jnp.log2(1.0 + jnp.exp2(-jnp.abs(z)))
            log_beta = jnp.minimum(z, 0.0) - sp
            log_1m = log_beta - z
            if mask is not None:
                log_1m = jnp.where(mask, log_1m, 0.0)
                log_beta = jnp.where(mask, log_beta, -jnp.inf)
            l1m_ref[hh] = log_1m.astype(BF16)
            lb_ref[hh] = log_beta
            rs_ref[hh] = jnp.broadcast_to(jnp.sum(log_1m, axis=-1, keepdims=True),
                                          rs_ref.shape[1:])

    def stage_b():
        for hh in range(heads_per_step):
            csum = csum_ref[hh]
            tail = (_dot(l1m_ref[hh], upper)
                    + jnp.concatenate([csum] * (block // V7X_LANES), axis=1))
            a_ref[hh] = jnp.exp2(lb_ref[hh] + tail).astype(BF16)
            csum_ref[hh] = csum + rs_ref[hh]

    def stage_c(blk):
        for hh, cols in enumerate(heads):
            acc_ref[hh] += _dot(a_ref[hh], v_ref[rows(blk), cols])

    acc_ref[...] = jnp.zeros_like(acc_ref)
    csum_ref[...] = jnp.zeros_like(csum_ref)
    stage_a(i, before)

    @pl.when(i >= 1)
    def _():
        stage_b()
        stage_a(i - 1, None)

    def body(t, carry):
        blk = i - t
        stage_c(blk + 2)
        stage_b()
        stage_a(blk, None)
        return carry

    lax.fori_loop(2, i + 1, body, 0)

    @pl.when(i >= 1)
    def _():
        stage_c(1)
        stage_b()

    @pl.when(i == 0)
    def _():
        stage_b()

    stage_c(0)
    for hh, cols in enumerate(heads):
        o_ref[:, cols] = acc_ref[hh].astype(o_ref.dtype)


def sb_attn(q, kv, *, n_heads, head_dim, block=256, heads_per_step=8):
    b, t, _ = q.shape
    blk = min(block, t)
    hp = heads_per_step
    assert t % blk == 0 and n_heads % hp == 0 and head_dim == V7X_LANES
    n_hblk = n_heads // hp
    wblk = hp * head_dim
    tile32 = hp * blk * blk * 4
    scratch_bytes = 2 * tile32 + 3 * hp * blk * V7X_LANES * 4
    vmem = 2 * t * wblk * 2 + 4 * blk * wblk * 2 + scratch_bytes + 4 * tile32

    def seq_spec(index_map):
        return pl.BlockSpec((None, t, wblk), index_map, pipeline_mode=pl.Buffered(1))

    return pl.pallas_call(
        functools.partial(_sb_kernel, block=blk, heads_per_step=hp, head_dim=head_dim),
        grid=(b, n_hblk, t // blk),
        in_specs=[pl.BlockSpec((None, blk, wblk), lambda bi, hi, qi: (bi, qi, hi)),
                  seq_spec(lambda bi, hi, qi: (bi, 0, hi)),
                  seq_spec(lambda bi, hi, qi: (bi, 0, n_hblk + hi))],
        out_specs=pl.BlockSpec((None, blk, wblk), lambda bi, hi, qi: (bi, qi, hi)),
        out_shape=jax.ShapeDtypeStruct((b, t, n_heads * head_dim), BF16),
        scratch_shapes=[pltpu.VMEM((hp, blk, blk), BF16),
                        pltpu.VMEM((hp, blk, blk), F32),
                        pltpu.VMEM((hp, blk, V7X_LANES), F32),
                        pltpu.VMEM((hp, blk, blk), BF16),
                        pltpu.VMEM((hp, blk, V7X_LANES), F32),
                        pltpu.VMEM((hp, blk, head_dim), F32)],
        compiler_params=pltpu.CompilerParams(
            dimension_semantics=("arbitrary", "arbitrary", "arbitrary"),
            vmem_limit_bytes=_vmem_limit(vmem)),
        name="sb_attn",
    )(q, kv, kv)


def kernel(x, mix_pre_gain, mix_post_gain, mlp_pre_gain, mlp_post_gain, mlp_w_up, mlp_w_down,
           gdn_w_in, gdn_conv_w, gdn_a_log, gdn_dt_bias, gdn_out_gain, gdn_w_out,
           kv_gain, w_kv, sb_w_q, sb_w_o):
    b, t, d = x.shape
    n = b * t
    n_a = gdn_w_in.shape[0]
    depth = mix_pre_gain.shape[0]
    gdn_heads = gdn_a_log.shape[1]
    gdn_dim = gdn_out_gain.shape[1]
    gdn_width = gdn_heads * gdn_dim
    sb_width = sb_w_q.shape[2]
    sb_dim = gdn_dim
    sb_heads = sb_width // sb_dim
    assert gdn_w_in.shape[2] == 4 * gdn_width + 2 * gdn_heads

    xs = x.reshape(n, d)
    kv = None
    for layer in range(depth):
        if layer < n_a:
            w_in = gdn_w_in[layer]
            w_main = w_in[:, :4 * gdn_width].astype(BF16)
            w_small = jnp.pad(w_in[:, 4 * gdn_width:],
                              ((0, 0), (0, V7X_LANES - 2 * gdn_heads))).astype(BF16)
            pre = mix_pre_gain[layer]
            proj, small = norm_proj(xs, [(pre, w_main, BF16, 1.0), (pre, w_small, F32, 1.0)])
            small = small[:, :2 * gdn_heads].reshape(b, t, 2, gdn_heads)
            ab = jnp.transpose(small[:, :, ::-1, :], (0, 3, 2, 1))
            o = gdn(proj.reshape(b, t, 4 * gdn_width), gdn_conv_w[layer], ab,
                    gdn_a_log[layer], gdn_dt_bias[layer], gdn_out_gain[layer],
                    n_heads=gdn_heads, head_dim=gdn_dim)
            w_o = gdn_w_out[layer]
        else:
            bl = layer - n_a
            q_scale = sb_dim ** -0.5 * math.log2(math.e)
            groups = [(mix_pre_gain[layer], sb_w_q[bl].astype(BF16), BF16, q_scale)]
            if kv is None:
                groups.append((kv_gain, w_kv.astype(BF16), BF16, 1.0))
                q, kv = norm_proj(xs, groups)
                kv = kv.reshape(b, t, 2 * sb_width)
            else:
                (q,) = norm_proj(xs, groups)
            o = sb_attn(q.reshape(b, t, sb_width), kv, n_heads=sb_heads, head_dim=sb_dim)
            w_o = sb_w_o[bl]
        xs = post_mlp(o.reshape(n, w_o.shape[0]), w_o.astype(BF16), mix_post_gain[layer], xs,
                      mlp_pre_gain[layer], mlp_w_up[layer].astype(BF16),
                      mlp_w_down[layer].astype(BF16), mlp_post_gain[layer])
    return xs.reshape(b, t, d)
```

```python
import functools
import math

import jax
import jax.numpy as jnp
import numpy as np
from jax import lax
from jax.experimental import pallas as pl
from jax.experimental.pallas import tpu as pltpu

F32 = jnp.float32
BF16 = jnp.bfloat16
EPS = 1e-6

V7X_LANES = 128
V7X_SUBLANES = 8
V7X_VMEM_BYTES = 64 * 1024 * 1024
V7X_VMEM_REQUEST_CAP = 56 * 1024 * 1024

GDN_BLOCK = 256
GDN_BASE = 8
CONV_TAPS = 4


def _vmem_limit(nbytes):
    return int(min(max(nbytes, 16 * 1024 * 1024), V7X_VMEM_REQUEST_CAP))


def _resident(shape, index_map):
    return pl.BlockSpec(shape, index_map, pipeline_mode=pl.Buffered(1))


def _rms_scale(x):
    return lax.rsqrt(jnp.mean(x * x, axis=-1, keepdims=True) + EPS)


def _dot(a, b):
    return jnp.dot(a, b, preferred_element_type=F32)


def _dot_nt(a, b):
    return lax.dot_general(a, b, (((1,), (1,)), ((), ())), preferred_element_type=F32)


def _dot_tn(a, b):
    return lax.dot_general(a, b, (((0,), (0,)), ((), ())), preferred_element_type=F32)


def _norm_proj_kernel(*refs, n_groups, scales, col_chunk):
    x_ref = refs[0]
    gain_refs = refs[1:1 + n_groups]
    w_refs = refs[1 + n_groups:1 + 2 * n_groups]
    out_refs = refs[1 + 2 * n_groups:]
    x = x_ref[...]
    xn = x * _rms_scale(x)
    for g in range(n_groups):
        h = (xn * gain_refs[g][...]).astype(BF16)
        n_cols = w_refs[g].shape[1]
        step = min(col_chunk, n_cols)
        for c in range(0, n_cols, step):
            acc = _dot(h, w_refs[g][:, c:c + step])
            if scales[g] != 1.0:
                acc = acc * scales[g]
            out_refs[g][:, c:c + step] = acc.astype(out_refs[g].dtype)


def norm_proj(x, groups, *, row_tile=512, col_chunk=1024):
    n, d = x.shape
    tm = min(row_tile, n)
    assert n % tm == 0
    n_groups = len(groups)
    gains = [g[0].reshape(1, d).astype(F32) for g in groups]
    ws = [g[1] for g in groups]
    in_specs = [pl.BlockSpec((tm, d), lambda i: (i, 0))]
    in_specs += [_resident((1, d), lambda i: (0, 0)) for _ in groups]
    in_specs += [_resident(w.shape, lambda i: (0, 0)) for w in ws]
    out_specs = [pl.BlockSpec((tm, w.shape[1]), lambda i: (i, 0)) for w in ws]
    out_shape = [jax.ShapeDtypeStruct((n, w.shape[1]), g[2]) for w, g in zip(ws, groups)]
    vmem = 2 * tm * d * 4 + sum(w.size * 2 for w in ws)
    vmem += sum(2 * tm * w.shape[1] * jnp.dtype(g[2]).itemsize for w, g in zip(ws, groups))
    vmem += 4 * tm * max(d, col_chunk) * 4
    return pl.pallas_call(
        functools.partial(_norm_proj_kernel, n_groups=n_groups,
                          scales=tuple(float(g[3]) for g in groups), col_chunk=col_chunk),
        grid=(n // tm,),
        in_specs=in_specs,
        out_specs=out_specs,
        out_shape=out_shape,
        compiler_params=pltpu.CompilerParams(
            dimension_semantics=("arbitrary",), vmem_limit_bytes=_vmem_limit(vmem)),
        name="norm_proj",
    )(x, *gains, *ws)


def _post_mlp_kernel(a_ref, wo_ref, gmix_ref, r_ref, gpre_ref, wup_ref, wdn_ref, gpost_ref, o_ref,
                     *, ff_chunk):
    mix = _dot(a_ref[...], wo_ref[...])
    x = r_ref[...] + mix * _rms_scale(mix) * gmix_ref[...]
    h = (x * _rms_scale(x) * gpre_ref[...]).astype(BF16)
    d_ff = wup_ref.shape[1]
    y = None
    for c in range(0, d_ff, ff_chunk):
        u = jnp.maximum(_dot(h, wup_ref[:, c:c + ff_chunk]), 0.0)
        part = _dot((u * u).astype(BF16), wdn_ref[c:c + ff_chunk, :])
        y = part if y is None else y + part
    o_ref[...] = x + y * _rms_scale(y) * gpost_ref[...]


def post_mlp(a, w_o, g_mix, resid, g_pre, w_up, w_down, g_post, *, row_tile=512, ff_chunk=1024):
    n, k = a.shape
    d = w_o.shape[1]
    d_ff = w_up.shape[1]
    tm = min(row_tile, n)
    fc = min(ff_chunk, d_ff)
    assert n % tm == 0 and d_ff % fc == 0

    def row_spec(width):
        return pl.BlockSpec((tm, width), lambda i: (i, 0))

    def gain(g):
        return g.reshape(1, d).astype(F32)

    vmem = 2 * tm * k * 2 + 4 * tm * d * 4 + (k * d + 2 * d * d_ff) * 2
    vmem += 3 * tm * fc * 4 + 4 * tm * d * 4
    return pl.pallas_call(
        functools.partial(_post_mlp_kernel, ff_chunk=fc),
        grid=(n // tm,),
        in_specs=[row_spec(k),
                  _resident((k, d), lambda i: (0, 0)),
                  _resident((1, d), lambda i: (0, 0)),
                  row_spec(d),
                  _resident((1, d), lambda i: (0, 0)),
                  _resident((d, d_ff), lambda i: (0, 0)),
                  _resident((d_ff, d), lambda i: (0, 0)),
                  _resident((1, d), lambda i: (0, 0))],
        out_specs=row_spec(d),
        out_shape=jax.ShapeDtypeStruct((n, d), F32),
        compiler_params=pltpu.CompilerParams(
            dimension_semantics=("arbitrary",), vmem_limit_bytes=_vmem_limit(vmem)),
        name="post_mlp",
    )(a, w_o, gain(g_mix), resid, gain(g_pre), w_up, w_down, gain(g_post))


def _conv_silu(x_bf16, w, win_ref):
    n = x_bf16.shape[0]
    x = x_bf16.astype(F32)
    win_ref[V7X_SUBLANES:, :] = x
    acc = x * w[CONV_TAPS - 1:CONV_TAPS, :]
    for s in range(1, CONV_TAPS):
        acc = acc + win_ref[pl.ds(V7X_SUBLANES - s, n), :] * w[CONV_TAPS - 1 - s:CONV_TAPS - s, :]
    win_ref[:V7X_SUBLANES, :] = x[n - V7X_SUBLANES:, :]
    return acc * jax.nn.sigmoid(acc)


def _softplus(x):
    return jnp.maximum(x, 0.0) + jnp.log(1.0 + jnp.exp(-jnp.abs(x)))


def _col_bcast(row, width):
    return jnp.broadcast_to(row, (width, row.shape[1])).T


MASK_INCL, MASK_EYE, MASK_BASE, MASK_MERGE0 = 0, 1, 2, 3


def _gdn_masks(n):
    ri = np.arange(n)[:, None]
    ci = np.arange(n)[None, :]
    masks = [ri >= ci, ri == ci, (ri // GDN_BASE == ci // GDN_BASE) & (ri > ci)]
    size = 2 * GDN_BASE
    while size <= n:
        masks.append((ri // size == ci // size) & (ri // (size // 2) > ci // (size // 2)))
        size *= 2
    return np.stack(masks).astype(np.float32)


def _inv_unit_lower(lows, masks_ref):
    eye = masks_ref[MASK_EYE]
    powers = [low * masks_ref[MASK_BASE] for low in lows]
    invs = [eye - p for p in powers]
    span = 2
    while span < GDN_BASE:
        powers = [_dot(p, p).astype(BF16) for p in powers]
        invs = [_dot(inv, eye + p).astype(BF16) for inv, p in zip(invs, powers)]
        span *= 2
    for level in range(MASK_MERGE0, masks_ref.shape[0]):
        halves = [_dot(inv, low * masks_ref[level]).astype(BF16) for inv, low in zip(invs, lows)]
        invs = [_dot(eye - h, inv).astype(BF16) for h, inv in zip(halves, invs)]
    return invs


def _gdn_head_inputs(hh, head, alog_ref, dtb_ref, q_ref, k_ref, v_ref, cwq_ref, cwk_ref, cwv_ref,
                     ab_ref, masks_ref, bias_ref, tail_ref, head_dim):
    tb = q_ref.shape[0]
    cols = slice(hh * head_dim, (hh + 1) * head_dim)
    q = _conv_silu(q_ref[:, cols], cwq_ref[:, cols], tail_ref.at[0, hh])
    k = _conv_silu(k_ref[:, cols], cwk_ref[:, cols], tail_ref.at[1, hh])
    v = _conv_silu(v_ref[:, cols], cwv_ref[:, cols], tail_ref.at[2, hh])
    q = q * (lax.rsqrt(jnp.sum(q * q, axis=-1, keepdims=True) + EPS) * head_dim ** -0.5)
    k = k * lax.rsqrt(jnp.sum(k * k, axis=-1, keepdims=True) + EPS)

    ab = ab_ref[hh]
    a_log = jnp.zeros((1, tb), F32) + alog_ref[head]
    g_row = -jnp.exp(a_log) * _softplus(ab[0:1] + dtb_ref[head])
    beta_row = jax.nn.sigmoid(ab[1:2])
    g_col = _col_bcast(g_row, head_dim)
    beta = _col_bcast(beta_row, head_dim)
    tril16 = masks_ref[MASK_INCL]
    g_hi = g_col.astype(BF16)
    g_lo = (g_col - g_hi.astype(F32)).astype(BF16)
    gc2 = _dot(tril16, jnp.concatenate([g_hi, g_lo], axis=1))
    gc = gc2[:, :head_dim] + gc2[:, head_dim:]
    gc_row = gc.T[0:1, :]
    gc_last = gc[tb - 1:tb, :]
    diff = jnp.concatenate([gc] * (tb // head_dim), axis=1) - gc_row
    decay = jnp.exp(diff + bias_ref[...])

    kb = k * beta
    qk = _dot_nt(jnp.concatenate([q, kb], axis=0).astype(BF16), k.astype(BF16))
    e_gc = jnp.exp(gc)
    return dict(
        attn16=(qk[:tb] * decay).astype(BF16),
        low16=(qk[tb:] * decay).astype(BF16),
        wu_rhs16=jnp.concatenate([kb * e_gc, v * beta], axis=1).astype(BF16),
        qg16=(q * e_gc).astype(BF16),
        kg16=(k * jnp.exp(gc_last - gc)).astype(BF16),
        g_last=jnp.exp(gc_last))


def _gdn_kernel(alog_ref, dtb_ref, q_ref, k_ref, v_ref, gate_ref, cwq_ref, cwk_ref, cwv_ref,
                ab_ref, og_ref, masks_ref, bias_ref, o_ref, state_ref, tail_ref,
                *, heads_per_step, head_dim):
    hblk = pl.program_id(1)
    tb = q_ref.shape[0]
    heads = range(heads_per_step)

    @pl.when(pl.program_id(2) == 0)
    def _():
        state_ref[...] = jnp.zeros_like(state_ref)
        tail_ref[...] = jnp.zeros_like(tail_ref)

    hd = [_gdn_head_inputs(hh, hblk * heads_per_step + hh, alog_ref, dtb_ref, q_ref, k_ref, v_ref,
                           cwq_ref, cwk_ref, cwv_ref, ab_ref, masks_ref, bias_ref, tail_ref,
                           head_dim)
          for hh in heads]
    t_mats = _inv_unit_lower([h["low16"] for h in hd], masks_ref)
    wus = [_dot(t, h["wu_rhs16"]) for t, h in zip(t_mats, hd)]
    states = [state_ref[hh] for hh in heads]
    ws_qs = [_dot(jnp.concatenate([wu[:, :head_dim].astype(BF16), h["qg16"]], axis=0),
                  s.astype(BF16)) for wu, h, s in zip(wus, hd, states)]
    v_new16 = [(wu[:, head_dim:] - wq[:tb]).astype(BF16) for wu, wq in zip(wus, ws_qs)]
    outs = [wq[tb:] + _dot(h["attn16"], vn) for wq, h, vn in zip(ws_qs, hd, v_new16)]
    for hh in heads:
        state_ref[hh] = states[hh] * hd[hh]["g_last"] + _dot_tn(hd[hh]["kg16"], v_new16[hh])
    for hh in heads:
        cols = slice(hh * head_dim, (hh + 1) * head_dim)
        gate = gate_ref[:, cols].astype(F32)
        o = outs[hh]
        o = o * _rms_scale(o) * og_ref[...] * (gate * jax.nn.sigmoid(gate))
        o_ref[:, cols] = o.astype(o_ref.dtype)


def gdn(proj, conv_w, ab, a_log, dt_bias, out_gain, *, n_heads, head_dim, heads_per_step=8):
    b, t, _ = proj.shape
    tb = min(GDN_BLOCK, t)
    assert t % tb == 0 and tb % head_dim == 0 and n_heads % heads_per_step == 0
    hb = heads_per_step
    n_hblk = n_heads // hb
    wblk = hb * head_dim
    masks_np = _gdn_masks(tb)
    masks = jnp.asarray(masks_np, dtype=BF16)
    bias = jnp.asarray(np.where(masks_np[MASK_INCL] > 0, 0.0, -np.inf), dtype=F32)
    vmem = masks.size * 2 + bias.size * 4 + 2 * 5 * tb * wblk * 2 + 40 * hb * tb * tb * 4

    def act_spec(section):
        return pl.BlockSpec((None, tb, wblk), lambda bi, hi, ti: (bi, ti, section * n_hblk + hi))

    def cw_spec(section):
        return pl.BlockSpec((CONV_TAPS, wblk), lambda bi, hi, ti: (0, section * n_hblk + hi))

    smem = pl.BlockSpec(memory_space=pltpu.SMEM)
    return pl.pallas_call(
        functools.partial(_gdn_kernel, heads_per_step=hb, head_dim=head_dim),
        grid=(b, n_hblk, t // tb),
        in_specs=[smem, smem,
                  act_spec(0), act_spec(1), act_spec(2), act_spec(3),
                  cw_spec(0), cw_spec(1), cw_spec(2),
                  pl.BlockSpec((None, hb, 2, tb), lambda bi, hi, ti: (bi, hi, 0, ti)),
                  pl.BlockSpec((1, head_dim), lambda bi, hi, ti: (0, 0)),
                  _resident(masks.shape, lambda bi, hi, ti: (0, 0, 0)),
                  _resident(bias.shape, lambda bi, hi, ti: (0, 0))],
        out_specs=pl.BlockSpec((None, tb, wblk), lambda bi, hi, ti: (bi, ti, hi)),
        out_shape=jax.ShapeDtypeStruct((b, t, n_heads * head_dim), BF16),
        scratch_shapes=[pltpu.VMEM((hb, head_dim, head_dim), F32),
                        pltpu.VMEM((3, hb, V7X_SUBLANES + tb, head_dim), F32)],
        compiler_params=pltpu.CompilerParams(
            dimension_semantics=("arbitrary", "arbitrary", "arbitrary"),
            vmem_limit_bytes=_vmem_limit(vmem)),
        name="gdn",
    )(a_log.astype(F32), dt_bias.astype(F32), proj, proj, proj, proj,
      conv_w, conv_w, conv_w, ab, out_gain.reshape(1, head_dim).astype(F32), masks, bias)


def _sb_kernel(q_ref, k_ref, v_ref, o_ref, l1m_ref, lb_ref, rs_ref, a_ref, csum_ref, acc_ref,
               *, block, heads_per_step, head_dim):
    i = pl.program_id(2)
    ri = lax.broadcasted_iota(jnp.int32, (block, block), 0)
    ci = lax.broadcasted_iota(jnp.int32, (block, block), 1)
    upper = (ri > ci).astype(BF16)
    before = ci < ri
    heads = [slice(hh * head_dim, (hh + 1) * head_dim) for hh in range(heads_per_step)]

    def rows(blk):
        return pl.ds(pl.multiple_of(blk * block, block), block)

    def stage_a(blk, mask):
        for hh, cols in enumerate(heads):
            z = _dot_nt(q_ref[:, cols], k_ref[rows(blk), cols])
            sp = jnp.log2(1.0 + jnp.exp2(-jnp.abs(z)))
            log_beta = jnp.minimum(z, 0.0) - sp
            log_1m = log_beta - z
            if mask is not None:
                log_1m = jnp.where(mask, log_1m, 0.0)
                log_beta = jnp.where(mask, log_beta, -jnp.inf)
            l1m_ref[hh] = log_1m.astype(BF16)
            lb_ref[hh] = log_beta
            rs_ref[hh] = jnp.broadcast_to(jnp.sum(log_1m, axis=-1, keepdims=True),
                                          rs_ref.shape[1:])

    def stage_b():
        for hh in range(heads_per_step):
            csum = csum_ref[hh]
            tail = (_dot(l1m_ref[hh], upper)
                    + jnp.concatenate([csum] * (block // V7X_LANES), axis=1))
            a_ref[hh] = jnp.exp2(lb_ref[hh] + tail).astype(BF16)
            csum_ref[hh] = csum + rs_ref[hh]

    def stage_c(blk):
        for hh, cols in enumerate(heads):
            acc_ref[hh] += _dot(a_ref[hh], v_ref[rows(blk), cols])

    acc_ref[...] = jnp.zeros_like(acc_ref)
    csum_ref[...] = jnp.zeros_like(csum_ref)
    stage_a(i, before)

    @pl.when(i >= 1)
    def _():
        stage_b()
        stage_a(i - 1, None)

    def body(t, carry):
        blk = i - t
        stage_c(blk + 2)
        stage_b()
        stage_a(blk, None)
        return carry

    lax.fori_loop(2, i + 1, body, 0)

    @pl.when(i >= 1)
    def _():
        stage_c(1)
        stage_b()

    @pl.when(i == 0)
    def _():
        stage_b()

    stage_c(0)
    for hh, cols in enumerate(heads):
        o_ref[:, cols] = acc_ref[hh].astype(o_ref.dtype)


def sb_attn(q, kv, *, n_heads, head_dim, block=256, heads_per_step=8):
    b, t, _ = q.shape
    blk = min(block, t)
    hp = heads_per_step
    assert t % blk == 0 and n_heads % hp == 0 and head_dim == V7X_LANES
    n_hblk = n_heads // hp
    wblk = hp * head_dim
    tile32 = hp * blk * blk * 4
    scratch_bytes = 2 * tile32 + 3 * hp * blk * V7X_LANES * 4
    vmem = 2 * t * wblk * 2 + 4 * blk * wblk * 2 + scratch_bytes + 4 * tile32

    def seq_spec(index_map):
        return pl.BlockSpec((None, t, wblk), index_map, pipeline_mode=pl.Buffered(1))

    return pl.pallas_call(
        functools.partial(_sb_kernel, block=blk, heads_per_step=hp, head_dim=head_dim),
        grid=(b, n_hblk, t // blk),
        in_specs=[pl.BlockSpec((None, blk, wblk), lambda bi, hi, qi: (bi, qi, hi)),
                  seq_spec(lambda bi, hi, qi: (bi, 0, hi)),
                  seq_spec(lambda bi, hi, qi: (bi, 0, n_hblk + hi))],
        out_specs=pl.BlockSpec((None, blk, wblk), lambda bi, hi, qi: (bi, qi, hi)),
        out_shape=jax.ShapeDtypeStruct((b, t, n_heads * head_dim), BF16),
        scratch_shapes=[pltpu.VMEM((hp, blk, blk), BF16),
                        pltpu.VMEM((hp, blk, blk), F32),
                        pltpu.VMEM((hp, blk, V7X_LANES), F32),
                        pltpu.VMEM((hp, blk, blk), BF16),
                        pltpu.VMEM((hp, blk, V7X_LANES), F32),
                        pltpu.VMEM((hp, blk, head_dim), F32)],
        compiler_params=pltpu.CompilerParams(
            dimension_semantics=("arbitrary", "arbitrary", "arbitrary"),
            vmem_limit_bytes=_vmem_limit(vmem)),
        name="sb_attn",
    )(q, kv, kv)


def kernel(x, mix_pre_gain, mix_post_gain, mlp_pre_gain, mlp_post_gain, mlp_w_up, mlp_w_down,
           gdn_w_in, gdn_conv_w, gdn_a_log, gdn_dt_bias, gdn_out_gain, gdn_w_out,
           kv_gain, w_kv, sb_w_q, sb_w_o):
    b, t, d = x.shape
    n = b * t
    n_a = gdn_w_in.shape[0]
    depth = mix_pre_gain.shape[0]
    gdn_heads = gdn_a_log.shape[1]
    gdn_dim = gdn_out_gain.shape[1]
    gdn_width = gdn_heads * gdn_dim
    sb_width = sb_w_q.shape[2]
    sb_dim = gdn_dim
    sb_heads = sb_width // sb_dim
    assert gdn_w_in.shape[2] == 4 * gdn_width + 2 * gdn_heads

    xs = x.reshape(n, d)
    kv = None
    for layer in range(depth):
        if layer < n_a:
            w_in = gdn_w_in[layer]
            w_main = w_in[:, :4 * gdn_width].astype(BF16)
            w_small = jnp.pad(w_in[:, 4 * gdn_width:],
                              ((0, 0), (0, V7X_LANES - 2 * gdn_heads))).astype(BF16)
            pre = mix_pre_gain[layer]
            proj, small = norm_proj(xs, [(pre, w_main, BF16, 1.0), (pre, w_small, F32, 1.0)])
            small = small[:, :2 * gdn_heads].reshape(b, t, 2, gdn_heads)
            ab = jnp.transpose(small[:, :, ::-1, :], (0, 3, 2, 1))
            o = gdn(proj.reshape(b, t, 4 * gdn_width), gdn_conv_w[layer], ab,
                    gdn_a_log[layer], gdn_dt_bias[layer], gdn_out_gain[layer],
                    n_heads=gdn_heads, head_dim=gdn_dim)
            w_o = gdn_w_out[layer]
        else:
            bl = layer - n_a
            q_scale = sb_dim ** -0.5 * math.log2(math.e)
            groups = [(mix_pre_gain[layer], sb_w_q[bl].astype(BF16), BF16, q_scale)]
            if kv is None:
                groups.append((kv_gain, w_kv.astype(BF16), BF16, 1.0))
                q, kv = norm_proj(xs, groups)
                kv = kv.reshape(b, t, 2 * sb_width)
            else:
                (q,) = norm_proj(xs, groups)
            o = sb_attn(q.reshape(b, t, sb_width), kv, n_heads=sb_heads, head_dim=sb_dim)
            w_o = sb_w_o[bl]
        xs = post_mlp(o.reshape(n, w_o.shape[0]), w_o.astype(BF16), mix_post_gain[layer], xs,
                      mlp_pre_gain[layer], mlp_w_up[layer].astype(BF16),
                      mlp_w_down[layer].astype(BF16), mlp_post_gain[layer])
    return xs.reshape(b, t, d)
```

```python
import functools
import math

import jax
import jax.numpy as jnp
import numpy as np
from jax import lax
from jax.experimental import pallas as pl
from jax.experimental.pallas import tpu as pltpu

F32 = jnp.float32
BF16 = jnp.bfloat16
EPS = 1e-6

V7X_LANES = 128
V7X_SUBLANES = 8
V7X_VMEM_BYTES = 64 * 1024 * 1024
V7X_VMEM_REQUEST_CAP = 56 * 1024 * 1024

GDN_BLOCK = 256
GDN_BASE = 8
CONV_TAPS = 4


def _vmem_limit(nbytes):
    return int(min(max(nbytes, 16 * 1024 * 1024), V7X_VMEM_REQUEST_CAP))


def _resident(shape, index_map):
    return pl.BlockSpec(shape, index_map, pipeline_mode=pl.Buffered(1))


def _rms_scale(x):
    return lax.rsqrt(jnp.mean(x * x, axis=-1, keepdims=True) + EPS)


def _dot(a, b):
    return jnp.dot(a, b, preferred_element_type=F32)


def _dot_nt(a, b):
    return lax.dot_general(a, b, (((1,), (1,)), ((), ())), preferred_element_type=F32)


def _dot_tn(a, b):
    return lax.dot_general(a, b, (((0,), (0,)), ((), ())), preferred_element_type=F32)


def _norm_proj_kernel(*refs, n_groups, scales, col_chunk):
    x_ref = refs[0]
    gain_refs = refs[1:1 + n_groups]
    w_refs = refs[1 + n_groups:1 + 2 * n_groups]
    out_refs = refs[1 + 2 * n_groups:]
    x = x_ref[...]
    xn = x * _rms_scale(x)
    for g in range(n_groups):
        h = (xn * gain_refs[g][...]).astype(BF16)
        n_cols = w_refs[g].shape[1]
        step = min(col_chunk, n_cols)
        for c in range(0, n_cols, step):
            acc = _dot(h, w_refs[g][:, c:c + step])
            if scales[g] != 1.0:
                acc = acc * scales[g]
            out_refs[g][:, c:c + step] = acc.astype(out_refs[g].dtype)


def norm_proj(x, groups, *, row_tile=512, col_chunk=1024):
    n, d = x.shape
    tm = min(row_tile, n)
    assert n % tm == 0
    n_groups = len(groups)
    gains = [g[0].reshape(1, d).astype(F32) for g in groups]
    ws = [g[1] for g in groups]
    in_specs = [pl.BlockSpec((tm, d), lambda i: (i, 0))]
    in_specs += [_resident((1, d), lambda i: (0, 0)) for _ in groups]
    in_specs += [_resident(w.shape, lambda i: (0, 0)) for w in ws]
    out_specs = [pl.BlockSpec((tm, w.shape[1]), lambda i: (i, 0)) for w in ws]
    out_shape = [jax.ShapeDtypeStruct((n, w.shape[1]), g[2]) for w, g in zip(ws, groups)]
    vmem = 2 * tm * d * 4 + sum(w.size * 2 for w in ws)
    vmem += sum(2 * tm * w.shape[1] * jnp.dtype(g[2]).itemsize for w, g in zip(ws, groups))
    vmem += 4 * tm * max(d, col_chunk) * 4
    return pl.pallas_call(
        functools.partial(_norm_proj_kernel, n_groups=n_groups,
                          scales=tuple(float(g[3]) for g in groups), col_chunk=col_chunk),
        grid=(n // tm,),
        in_specs=in_specs,
        out_specs=out_specs,
        out_shape=out_shape,
        compiler_params=pltpu.CompilerParams(
            dimension_semantics=("arbitrary",), vmem_limit_bytes=_vmem_limit(vmem)),
        name="norm_proj",
    )(x, *gains, *ws)


def _post_mlp_kernel(a_ref, wo_ref, gmix_ref, r_ref, gpre_ref, wup_ref, wdn_ref, gpost_ref, o_ref,
                     *, ff_chunk):
    mix = _dot(a_ref[...], wo_ref[...])
    x = r_ref[...] + mix * _rms_scale(mix) * gmix_ref[...]
    h = (x * _rms_scale(x) * gpre_ref[...]).astype(BF16)
    d_ff = wup_ref.shape[1]
    y = None
    for c in range(0, d_ff, ff_chunk):
        u = jnp.maximum(_dot(h, wup_ref[:, c:c + ff_chunk]), 0.0)
        part = _dot((u * u).astype(BF16), wdn_ref[c:c + ff_chunk, :])
        y = part if y is None else y + part
    o_ref[...] = x + y * _rms_scale(y) * gpost_ref[...]


def post_mlp(a, w_o, g_mix, resid, g_pre, w_up, w_down, g_post, *, row_tile=512, ff_chunk=1024):
    n, k = a.shape
    d = w_o.shape[1]
    d_ff = w_up.shape[1]
    tm = min(row_tile, n)
    fc = min(ff_chunk, d_ff)
    assert n % tm == 0 and d_ff % fc == 0

    def row_spec(width):
        return pl.BlockSpec((tm, width), lambda i: (i, 0))

    def gain(g):
        return g.reshape(1, d).astype(F32)

    vmem = 2 * tm * k * 2 + 4 * tm * d * 4 + (k * d + 2 * d * d_ff) * 2
    vmem += 3 * tm * fc * 4 + 4 * tm * d * 4
    return pl.pallas_call(
        functools.partial(_post_mlp_kernel, ff_chunk=fc),
        grid=(n // tm,),
        in_specs=[row_spec(k),
                  _resident((k, d), lambda i: (0, 0)),
                  _resident((1, d), lambda i: (0, 0)),
                  row_spec(d),
                  _resident((1, d), lambda i: (0, 0)),
                  _resident((d, d_ff), lambda i: (0, 0)),
                  _resident((d_ff, d), lambda i: (0, 0)),
                  _resident((1, d), lambda i: (0, 0))],
        out_specs=row_spec(d),
        out_shape=jax.ShapeDtypeStruct((n, d), F32),
        compiler_params=pltpu.CompilerParams(
            dimension_semantics=("arbitrary",), vmem_limit_bytes=_vmem_limit(vmem)),
        name="post_mlp",
    )(a, w_o, gain(g_mix), resid, gain(g_pre), w_up, w_down, gain(g_post))


def _conv_silu(x_bf16, w, win_ref):
    n = x_bf16.shape[0]
    x = x_bf16.astype(F32)
    win_ref[V7X_SUBLANES:, :] = x
    acc = x * w[CONV_TAPS - 1:CONV_TAPS, :]
    for s in range(1, CONV_TAPS):
        acc = acc + win_ref[pl.ds(V7X_SUBLANES - s, n), :] * w[CONV_TAPS - 1 - s:CONV_TAPS - s, :]
    win_ref[:V7X_SUBLANES, :] = x[n - V7X_SUBLANES:, :]
    return acc * jax.nn.sigmoid(acc)


def _softplus(x):
    return jnp.maximum(x, 0.0) + jnp.log(1.0 + jnp.exp(-jnp.abs(x)))


def _col_bcast(row, width):
    return jnp.broadcast_to(row, (width, row.shape[1])).T


MASK_INCL, MASK_EYE, MASK_BASE, MASK_MERGE0 = 0, 1, 2, 3


def _gdn_masks(n):
    ri = np.arange(n)[:, None]
    ci = np.arange(n)[None, :]
    masks = [ri >= ci, ri == ci, (ri // GDN_BASE == ci // GDN_BASE) & (ri > ci)]
    size = 2 * GDN_BASE
    while size <= n:
        masks.append((ri // size == ci // size) & (ri // (size // 2) > ci // (size // 2)))
        size *= 2
    return np.stack(masks).astype(np.float32)


def _inv_unit_lower(lows, masks_ref, between_levels):
    eye = masks_ref[MASK_EYE]
    powers = [low * masks_ref[MASK_BASE] for low in lows]
    invs = [eye - p for p in powers]
    span = 2
    while span < GDN_BASE:
        powers = [_dot(p, p).astype(BF16) for p in powers]
        invs = [_dot(inv, eye + p).astype(BF16) for inv, p in zip(invs, powers)]
        between_levels()
        span *= 2
    for level in range(MASK_MERGE0, masks_ref.shape[0]):
        halves = [_dot(inv, low * masks_ref[level]).astype(BF16) for inv, low in zip(invs, lows)]
        invs = [_dot(eye - h, inv).astype(BF16) for h, inv in zip(halves, invs)]
        between_levels()
    return invs


def _gdn_head_inputs(hh, head, alog_ref, dtb_ref, q_ref, k_ref, v_ref, cwq_ref, cwk_ref, cwv_ref,
                     ab_ref, masks_ref, bias_ref, tail_ref, head_dim):
    tb = q_ref.shape[0]
    cols = slice(hh * head_dim, (hh + 1) * head_dim)
    q = _conv_silu(q_ref[:, cols], cwq_ref[:, cols], tail_ref.at[0, hh])
    k = _conv_silu(k_ref[:, cols], cwk_ref[:, cols], tail_ref.at[1, hh])
    v = _conv_silu(v_ref[:, cols], cwv_ref[:, cols], tail_ref.at[2, hh])
    q = q * (lax.rsqrt(jnp.sum(q * q, axis=-1, keepdims=True) + EPS) * head_dim ** -0.5)
    k = k * lax.rsqrt(jnp.sum(k * k, axis=-1, keepdims=True) + EPS)

    ab = ab_ref[hh]
    a_log = jnp.zeros((1, tb), F32) + alog_ref[head]
    g_row = -jnp.exp(a_log) * _softplus(ab[0:1] + dtb_ref[head])
    beta_row = jax.nn.sigmoid(ab[1:2])
    g_col = _col_bcast(g_row, head_dim)
    beta = _col_bcast(beta_row, head_dim)
    tril16 = masks_ref[MASK_INCL]
    g_hi = g_col.astype(BF16)
    g_lo = (g_col - g_hi.astype(F32)).astype(BF16)
    gc2 = _dot(tril16, jnp.concatenate([g_hi, g_lo], axis=1))
    gc = gc2[:, :head_dim] + gc2[:, head_dim:]
    gc_row = gc.T[0:1, :]
    gc_last = gc[tb - 1:tb, :]
    diff = jnp.concatenate([gc] * (tb // head_dim), axis=1) - gc_row
    decay = jnp.exp(diff + bias_ref[...])

    kb = k * beta
    qk = _dot_nt(jnp.concatenate([q, kb], axis=0).astype(BF16), k.astype(BF16))
    e_gc = jnp.exp(gc)
    return dict(
        attn16=(qk[:tb] * decay).astype(BF16),
        low16=(qk[tb:] * decay).astype(BF16),
        wu_rhs16=jnp.concatenate([kb * e_gc, v * beta], axis=1).astype(BF16),
        qg16=(q * e_gc).astype(BF16),
        kg16=(k * jnp.exp(gc_last - gc)).astype(BF16),
        g_last=jnp.exp(gc_last))


def _gdn_kernel(alog_ref, dtb_ref, q_ref, k_ref, v_ref, gate_ref, cwq_ref, cwk_ref, cwv_ref,
                ab_ref, og_ref, masks_ref, bias_ref, o_ref, state_ref, tail_ref,
                attn_ref, low_ref, wu_rhs_ref, qg_ref, kg_ref, glast_ref,
                *, heads_per_step, head_dim):
    hblk = pl.program_id(1)
    step = pl.program_id(2)
    tb = q_ref.shape[0]
    heads = range(heads_per_step)
    handover = dict(attn16=attn_ref, low16=low_ref, wu_rhs16=wu_rhs_ref, qg16=qg_ref,
                    kg16=kg_ref, g_last=glast_ref)

    @pl.when(step == 0)
    def _():
        state_ref[...] = jnp.zeros_like(state_ref)
        tail_ref[...] = jnp.zeros_like(tail_ref)
        for ref in handover.values():
            ref[...] = jnp.zeros_like(ref)

    def run(read, write):
        to_prepare = list(heads)

        def prepare_one():
            if to_prepare:
                hh = to_prepare.pop(0)
                prepared = _gdn_head_inputs(hh, hblk * heads_per_step + hh, alog_ref, dtb_ref,
                                            q_ref, k_ref, v_ref, cwq_ref, cwk_ref, cwv_ref,
                                            ab_ref, masks_ref, bias_ref, tail_ref, head_dim)
                for name, ref in handover.items():
                    ref[write, hh] = prepared[name]

        hd = [{name: ref[read, hh] for name, ref in handover.items()} for hh in heads]
        t_mats = _inv_unit_lower([h["low16"] for h in hd], masks_ref, prepare_one)
        wus = [_dot(t, h["wu_rhs16"]) for t, h in zip(t_mats, hd)]
        prepare_one()
        states = [state_ref[hh] for hh in heads]
        ws_qs = [_dot(jnp.concatenate([wu[:, :head_dim].astype(BF16), h["qg16"]], axis=0),
                      s.astype(BF16)) for wu, h, s in zip(wus, hd, states)]
        v_new16 = [(wu[:, head_dim:] - wq[:tb]).astype(BF16) for wu, wq in zip(wus, ws_qs)]
        outs = [wq[tb:] + _dot(h["attn16"], vn) for wq, h, vn in zip(ws_qs, hd, v_new16)]
        for hh in heads:
            state_ref[hh] = states[hh] * hd[hh]["g_last"] + _dot_tn(hd[hh]["kg16"], v_new16[hh])
        for hh in heads:
            cols = slice(hh * head_dim, (hh + 1) * head_dim)
            gate = gate_ref[:, cols].astype(F32)
            o = outs[hh]
            o = o * _rms_scale(o) * og_ref[...] * (gate * jax.nn.sigmoid(gate))
            o_ref[:, cols] = o.astype(o_ref.dtype)
        while to_prepare:
            prepare_one()

    @pl.when((step & 1) == 0)
    def _():
        run(1, 0)

    @pl.when((step & 1) == 1)
    def _():
        run(0, 1)


def gdn(proj, conv_w, ab, a_log, dt_bias, out_gain, *, n_heads, head_dim, heads_per_step=8):
    b, t, _ = proj.shape
    tb = min(GDN_BLOCK, t)
    assert t % tb == 0 and tb % head_dim == 0 and n_heads % heads_per_step == 0
    hb = heads_per_step
    n_hblk = n_heads // hb
    wblk = hb * head_dim
    masks_np = _gdn_masks(tb)
    masks = jnp.asarray(masks_np, dtype=BF16)
    bias = jnp.asarray(np.where(masks_np[MASK_INCL] > 0, 0.0, -np.inf), dtype=F32)
    vmem = masks.size * 2 + bias.size * 4 + 2 * 5 * tb * wblk * 2 + 40 * hb * tb * tb * 4

    n_tb = t // tb

    def prep_blk(ti):
        return jnp.minimum(ti, n_tb - 1)

    def solve_blk(ti):
        return jnp.maximum(ti - 1, 0)

    def act_spec(section, which):
        return pl.BlockSpec((None, tb, wblk),
                            lambda bi, hi, ti: (bi, which(ti), section * n_hblk + hi))

    def cw_spec(section):
        return pl.BlockSpec((CONV_TAPS, wblk), lambda bi, hi, ti: (0, section * n_hblk + hi))

    smem = pl.BlockSpec(memory_space=pltpu.SMEM)
    return pl.pallas_call(
        functools.partial(_gdn_kernel, heads_per_step=hb, head_dim=head_dim),
        grid=(b, n_hblk, n_tb + 1),
        in_specs=[smem, smem,
                  act_spec(0, prep_blk), act_spec(1, prep_blk), act_spec(2, prep_blk),
                  act_spec(3, solve_blk),
                  cw_spec(0), cw_spec(1), cw_spec(2),
                  pl.BlockSpec((None, hb, 2, tb), lambda bi, hi, ti: (bi, hi, 0, prep_blk(ti))),
                  pl.BlockSpec((1, head_dim), lambda bi, hi, ti: (0, 0)),
                  _resident(masks.shape, lambda bi, hi, ti: (0, 0, 0)),
                  _resident(bias.shape, lambda bi, hi, ti: (0, 0))],
        out_specs=pl.BlockSpec((None, tb, wblk), lambda bi, hi, ti: (bi, solve_blk(ti), hi)),
        out_shape=jax.ShapeDtypeStruct((b, t, n_heads * head_dim), BF16),
        scratch_shapes=[pltpu.VMEM((hb, head_dim, head_dim), F32),
                        pltpu.VMEM((3, hb, V7X_SUBLANES + tb, head_dim), F32),
                        pltpu.VMEM((2, hb, tb, tb), BF16),
                        pltpu.VMEM((2, hb, tb, tb), BF16),
                        pltpu.VMEM((2, hb, tb, 2 * head_dim), BF16),
                        pltpu.VMEM((2, hb, tb, head_dim), BF16),
                        pltpu.VMEM((2, hb, tb, head_dim), BF16),
                        pltpu.VMEM((2, hb, 1, head_dim), F32)],
        compiler_params=pltpu.CompilerParams(
            dimension_semantics=("arbitrary", "arbitrary", "arbitrary"),
            vmem_limit_bytes=_vmem_limit(vmem)),
        name="gdn",
    )(a_log.astype(F32), dt_bias.astype(F32), proj, proj, proj, proj,
      conv_w, conv_w, conv_w, ab, out_gain.reshape(1, head_dim).astype(F32), masks, bias)


def _sb_kernel(q_ref, k_ref, v_ref, o_ref, l1m_ref, lb_ref, rs_ref, a_ref, csum_ref, acc_ref,
               *, block, heads_per_step, head_dim):
    i = pl.program_id(2)
    ri = lax.broadcasted_iota(jnp.int32, (block, block), 0)
    ci = lax.broadcasted_iota(jnp.int32, (block, block), 1)
    upper = (ri > ci).astype(BF16)
    before = ci < ri
    heads = [slice(hh * head_dim, (hh + 1) * head_dim) for hh in range(heads_per_step)]

    def rows(blk):
        return pl.ds(pl.multiple_of(blk * block, block), block)

    def stage_a(blk, mask):
        for hh, cols in enumerate(heads):
            z = _dot_nt(q_ref[:, cols], k_ref[rows(blk), cols])
            sp = jnp.log2(1.0 + jnp.exp2(-jnp.abs(z)))
            log_beta = jnp.minimum(z, 0.0) - sp
            log_1m = log_beta - z
            if mask is not None:
                log_1m = jnp.where(mask, log_1m, 0.0)
                log_beta = jnp.where(mask, log_beta, -jnp.inf)
            l1m_ref[hh] = log_1m.astype(BF16)
            lb_ref[hh] = log_beta
            rs_ref[hh] = jnp.broadcast_to(jnp.sum(log_1m, axis=-1, keepdims=True),
                                          rs_ref.shape[1:])

    def stage_b():
        for hh in range(heads_per_step):
            csum = csum_ref[hh]
            tail = (_dot(l1m_ref[hh], upper)
                    + jnp.concatenate([csum] * (block // V7X_LANES), axis=1))
            a_ref[hh] = jnp.exp2(lb_ref[hh] + tail).astype(BF16)
            csum_ref[hh] = csum + rs_ref[hh]

    def stage_c(blk):
        for hh, cols in enumerate(heads):
            acc_ref[hh] += _dot(a_ref[hh], v_ref[rows(blk), cols])

    acc_ref[...] = jnp.zeros_like(acc_ref)
    csum_ref[...] = jnp.zeros_like(csum_ref)
    stage_a(i, before)

    @pl.when(i >= 1)
    def _():
        stage_b()
        stage_a(i - 1, None)

    def body(t, carry):
        blk = i - t
        stage_c(blk + 2)
        stage_b()
        stage_a(blk, None)
        return carry

    lax.fori_loop(2, i + 1, body, 0)

    @pl.when(i >= 1)
    def _():
        stage_c(1)
        stage_b()

    @pl.when(i == 0)
    def _():
        stage_b()

    stage_c(0)
    for hh, cols in enumerate(heads):
        o_ref[:, cols] = acc_ref[hh].astype(o_ref.dtype)


def sb_attn(q, kv, *, n_heads, head_dim, block=256, heads_per_step=8):
    b, t, _ = q.shape
    blk = min(block, t)
    hp = heads_per_step
    assert t % blk == 0 and n_heads % hp == 0 and head_dim == V7X_LANES
    n_hblk = n_heads // hp
    wblk = hp * head_dim
    tile32 = hp * blk * blk * 4
    scratch_bytes = 2 * tile32 + 3 * hp * blk * V7X_LANES * 4
    vmem = 2 * t * wblk * 2 + 4 * blk * wblk * 2 + scratch_bytes + 4 * tile32

    def seq_spec(index_map):
        return pl.BlockSpec((None, t, wblk), index_map, pipeline_mode=pl.Buffered(1))

    return pl.pallas_call(
        functools.partial(_sb_kernel, block=blk, heads_per_step=hp, head_dim=head_dim),
        grid=(b, n_hblk, t // blk),
        in_specs=[pl.BlockSpec((None, blk, wblk), lambda bi, hi, qi: (bi, qi, hi)),
                  seq_spec(lambda bi, hi, qi: (bi, 0, hi)),
                  seq_spec(lambda bi, hi, qi: (bi, 0, n_hblk + hi))],
        out_specs=pl.BlockSpec((None, blk, wblk), lambda bi, hi, qi: (bi, qi, hi)),
        out_shape=jax.ShapeDtypeStruct((b, t, n_heads * head_dim), BF16),
        scratch_shapes=[pltpu.VMEM((hp, blk, blk), BF16),
                        pltpu.VMEM((hp, blk, blk), F32),
                        pltpu.VMEM((hp, blk, V7X_LANES), F32),
                        pltpu.VMEM((hp, blk, blk), BF16),
                        pltpu.VMEM((hp, blk, V7X_LANES), F32),
                        pltpu.VMEM((hp, blk, head_dim), F32)],
        compiler_params=pltpu.CompilerParams(
            dimension_semantics=("arbitrary", "arbitrary", "arbitrary"),
            vmem_limit_bytes=_vmem_limit(vmem)),
        name="sb_attn",
    )(q, kv, kv)


def kernel(x, mix_pre_gain, mix_post_gain, mlp_pre_gain, mlp_post_gain, mlp_w_up, mlp_w_down,
           gdn_w_in, gdn_conv_w, gdn_a_log, gdn_dt_bias, gdn_out_gain, gdn_w_out,
           kv_gain, w_kv, sb_w_q, sb_w_o):
    b, t, d = x.shape
    n = b * t
    n_a = gdn_w_in.shape[0]
    depth = mix_pre_gain.shape[0]
    gdn_heads = gdn_a_log.shape[1]
    gdn_dim = gdn_out_gain.shape[1]
    gdn_width = gdn_heads * gdn_dim
    sb_width = sb_w_q.shape[2]
    sb_dim = gdn_dim
    sb_heads = sb_width // sb_dim
    assert gdn_w_in.shape[2] == 4 * gdn_width + 2 * gdn_heads

    xs = x.reshape(n, d)
    kv = None
    for layer in range(depth):
        if layer < n_a:
            w_in = gdn_w_in[layer]
            w_main = w_in[:, :4 * gdn_width].astype(BF16)
            w_small = jnp.concatenate(
                [w_in[:, 4 * gdn_width + gdn_heads:], w_in[:, 4 * gdn_width:4 * gdn_width + gdn_heads],
                 jnp.zeros((d, V7X_LANES - 2 * gdn_heads), w_in.dtype)], axis=1).astype(BF16)
            pre = mix_pre_gain[layer]
            proj, small = norm_proj(xs, [(pre, w_main, BF16, 1.0), (pre, w_small, F32, 1.0)])
            small = small[:, :2 * gdn_heads].reshape(b, t, 2, gdn_heads)
            ab = jnp.transpose(small, (0, 3, 2, 1))
            o = gdn(proj.reshape(b, t, 4 * gdn_width), gdn_conv_w[layer], ab,
                    gdn_a_log[layer], gdn_dt_bias[layer], gdn_out_gain[layer],
                    n_heads=gdn_heads, head_dim=gdn_dim)
            w_o = gdn_w_out[layer]
        else:
            bl = layer - n_a
            q_scale = sb_dim ** -0.5 * math.log2(math.e)
            groups = [(mix_pre_gain[layer], sb_w_q[bl].astype(BF16), BF16, q_scale)]
            if kv is None:
                groups.append((kv_gain, w_kv.astype(BF16), BF16, 1.0))
                q, kv = norm_proj(xs, groups)
                kv = kv.reshape(b, t, 2 * sb_width)
            else:
                (q,) = norm_proj(xs, groups)
            o = sb_attn(q.reshape(b, t, sb_width), kv, n_heads=sb_heads, head_dim=sb_dim)
            w_o = sb_w_o[bl]
        xs = post_mlp(o.reshape(n, w_o.shape[0]), w_o.astype(BF16), mix_post_gain[layer], xs,
                      mlp_pre_gain[layer], mlp_w_up[layer].astype(BF16),
                      mlp_w_down[layer].astype(BF16), mlp_post_gain[layer])
    return xs.reshape(b, t, d)
```

```python
import functools
import math

import jax
import jax.numpy as jnp
import numpy as np
from jax import lax
from jax.experimental import pallas as pl
from jax.experimental.pallas import tpu as pltpu

F32 = jnp.float32
BF16 = jnp.bfloat16
EPS = 1e-6

V7X_LANES = 128
V7X_SUBLANES = 8
V7X_VMEM_BYTES = 64 * 1024 * 1024
V7X_VMEM_REQUEST_CAP = 56 * 1024 * 1024

GDN_BLOCK = 256
GDN_BASE = 8
CONV_TAPS = 4


def _vmem_limit(nbytes):
    return int(min(max(nbytes, 16 * 1024 * 1024), V7X_VMEM_REQUEST_CAP))


def _resident(shape, index_map):
    return pl.BlockSpec(shape, index_map, pipeline_mode=pl.Buffered(1))


def _rms_scale(x):
    return lax.rsqrt(jnp.mean(x * x, axis=-1, keepdims=True) + EPS)


def _dot(a, b):
    return jnp.dot(a, b, preferred_element_type=F32)


def _dot_nt(a, b):
    return lax.dot_general(a, b, (((1,), (1,)), ((), ())), preferred_element_type=F32)


def _dot_tn(a, b):
    return lax.dot_general(a, b, (((0,), (0,)), ((), ())), preferred_element_type=F32)


def _norm_proj_kernel(*refs, n_groups, scales, col_chunk):
    x_ref = refs[0]
    gain_refs = refs[1:1 + n_groups]
    w_refs = refs[1 + n_groups:1 + 2 * n_groups]
    out_refs = refs[1 + 2 * n_groups:]
    x = x_ref[...]
    xn = x * _rms_scale(x)
    for g in range(n_groups):
        h = (xn * gain_refs[g][...]).astype(BF16)
        n_cols = w_refs[g].shape[1]
        step = min(col_chunk, n_cols)
        for c in range(0, n_cols, step):
            acc = _dot(h, w_refs[g][:, c:c + step])
            if scales[g] != 1.0:
                acc = acc * scales[g]
            out_refs[g][:, c:c + step] = acc.astype(out_refs[g].dtype)


def norm_proj(x, groups, *, row_tile=512, col_chunk=1024):
    n, d = x.shape
    tm = min(row_tile, n)
    assert n % tm == 0
    n_groups = len(groups)
    gains = [g[0].reshape(1, d).astype(F32) for g in groups]
    ws = [g[1] for g in groups]
    in_specs = [pl.BlockSpec((tm, d), lambda i: (i, 0))]
    in_specs += [_resident((1, d), lambda i: (0, 0)) for _ in groups]
    in_specs += [_resident(w.shape, lambda i: (0, 0)) for w in ws]
    out_specs = [pl.BlockSpec((tm, w.shape[1]), lambda i: (i, 0)) for w in ws]
    out_shape = [jax.ShapeDtypeStruct((n, w.shape[1]), g[2]) for w, g in zip(ws, groups)]
    vmem = 2 * tm * d * 4 + sum(w.size * 2 for w in ws)
    vmem += sum(2 * tm * w.shape[1] * jnp.dtype(g[2]).itemsize for w, g in zip(ws, groups))
    vmem += 4 * tm * max(d, col_chunk) * 4
    return pl.pallas_call(
        functools.partial(_norm_proj_kernel, n_groups=n_groups,
                          scales=tuple(float(g[3]) for g in groups), col_chunk=col_chunk),
        grid=(n // tm,),
        in_specs=in_specs,
        out_specs=out_specs,
        out_shape=out_shape,
        compiler_params=pltpu.CompilerParams(
            dimension_semantics=("arbitrary",), vmem_limit_bytes=_vmem_limit(vmem)),
        name="norm_proj",
    )(x, *gains, *ws)


def _post_mlp_kernel(a_ref, wo_ref, gmix_ref, r_ref, gpre_ref, wup_ref, wdn_ref, gpost_ref, o_ref,
                     *, ff_chunk):
    mix = _dot(a_ref[...], wo_ref[...])
    x = r_ref[...] + mix * _rms_scale(mix) * gmix_ref[...]
    h = (x * _rms_scale(x) * gpre_ref[...]).astype(BF16)
    d_ff = wup_ref.shape[1]
    y = None
    for c in range(0, d_ff, ff_chunk):
        u = jnp.maximum(_dot(h, wup_ref[:, c:c + ff_chunk]), 0.0)
        part = _dot((u * u).astype(BF16), wdn_ref[c:c + ff_chunk, :])
        y = part if y is None else y + part
    o_ref[...] = x + y * _rms_scale(y) * gpost_ref[...]


def post_mlp(a, w_o, g_mix, resid, g_pre, w_up, w_down, g_post, *, row_tile=512, ff_chunk=1024):
    n, k = a.shape
    d = w_o.shape[1]
    d_ff = w_up.shape[1]
    tm = min(row_tile, n)
    fc = min(ff_chunk, d_ff)
    assert n % tm == 0 and d_ff % fc == 0

    def row_spec(width):
        return pl.BlockSpec((tm, width), lambda i: (i, 0))

    def gain(g):
        return g.reshape(1, d).astype(F32)

    vmem = 2 * tm * k * 2 + 4 * tm * d * 4 + (k * d + 2 * d * d_ff) * 2
    vmem += 3 * tm * fc * 4 + 4 * tm * d * 4
    return pl.pallas_call(
        functools.partial(_post_mlp_kernel, ff_chunk=fc),
        grid=(n // tm,),
        in_specs=[row_spec(k),
                  _resident((k, d), lambda i: (0, 0)),
                  _resident((1, d), lambda i: (0, 0)),
                  row_spec(d),
                  _resident((1, d), lambda i: (0, 0)),
                  _resident((d, d_ff), lambda i: (0, 0)),
                  _resident((d_ff, d), lambda i: (0, 0)),
                  _resident((1, d), lambda i: (0, 0))],
        out_specs=row_spec(d),
        out_shape=jax.ShapeDtypeStruct((n, d), F32),
        compiler_params=pltpu.CompilerParams(
            dimension_semantics=("arbitrary",), vmem_limit_bytes=_vmem_limit(vmem)),
        name="post_mlp",
    )(a, w_o, gain(g_mix), resid, gain(g_pre), w_up, w_down, gain(g_post))


def _conv_silu(x_bf16, w, win_ref):
    n = x_bf16.shape[0]
    x = x_bf16.astype(F32)
    win_ref[V7X_SUBLANES:, :] = x
    acc = x * w[CONV_TAPS - 1:CONV_TAPS, :]
    for s in range(1, CONV_TAPS):
        acc = acc + win_ref[pl.ds(V7X_SUBLANES - s, n), :] * w[CONV_TAPS - 1 - s:CONV_TAPS - s, :]
    win_ref[:V7X_SUBLANES, :] = x[n - V7X_SUBLANES:, :]
    return acc * jax.nn.sigmoid(acc)


def _softplus(x):
    return jnp.maximum(x, 0.0) + jnp.log(1.0 + jnp.exp(-jnp.abs(x)))


def _col_bcast(row, width):
    return jnp.broadcast_to(row, (width, row.shape[1])).T


MASK_INCL, MASK_EYE, MASK_BASE, MASK_MERGE0 = 0, 1, 2, 3


def _gdn_masks(n):
    ri = np.arange(n)[:, None]
    ci = np.arange(n)[None, :]
    masks = [ri >= ci, ri == ci, (ri // GDN_BASE == ci // GDN_BASE) & (ri > ci)]
    size = 2 * GDN_BASE
    while size <= n:
        masks.append((ri // size == ci // size) & (ri // (size // 2) > ci // (size // 2)))
        size *= 2
    return np.stack(masks).astype(np.float32)


def _inv_unit_lower(lows, masks_ref, between_levels):
    eye = masks_ref[MASK_EYE]
    powers = [low * masks_ref[MASK_BASE] for low in lows]
    invs = [eye - p for p in powers]
    span = 2
    while span < GDN_BASE:
        powers = [_dot(p, p).astype(BF16) for p in powers]
        invs = [_dot(inv, eye + p).astype(BF16) for inv, p in zip(invs, powers)]
        between_levels()
        span *= 2
    for level in range(MASK_MERGE0, masks_ref.shape[0]):
        halves = [_dot(inv, low * masks_ref[level]).astype(BF16) for inv, low in zip(invs, lows)]
        invs = [_dot(eye - h, inv).astype(BF16) for h, inv in zip(halves, invs)]
        between_levels()
    return invs


def _gdn_head_inputs(hh, head, alog_ref, dtb_ref, q_ref, k_ref, v_ref, cwq_ref, cwk_ref, cwv_ref,
                     ab_ref, masks_ref, bias_ref, tail_ref, head_dim):
    tb = q_ref.shape[0]
    cols = slice(hh * head_dim, (hh + 1) * head_dim)
    q = _conv_silu(q_ref[:, cols], cwq_ref[:, cols], tail_ref.at[0, hh])
    k = _conv_silu(k_ref[:, cols], cwk_ref[:, cols], tail_ref.at[1, hh])
    v = _conv_silu(v_ref[:, cols], cwv_ref[:, cols], tail_ref.at[2, hh])
    q = q * (lax.rsqrt(jnp.sum(q * q, axis=-1, keepdims=True) + EPS) * head_dim ** -0.5)
    k = k * lax.rsqrt(jnp.sum(k * k, axis=-1, keepdims=True) + EPS)

    ab = ab_ref[hh]
    a_log = jnp.zeros((1, tb), F32) + alog_ref[head]
    g_row = -jnp.exp(a_log) * _softplus(ab[0:1] + dtb_ref[head])
    beta_row = jax.nn.sigmoid(ab[1:2])
    g_col = _col_bcast(g_row, head_dim)
    beta = _col_bcast(beta_row, head_dim)
    tril16 = masks_ref[MASK_INCL]
    g_hi = g_col.astype(BF16)
    g_lo = (g_col - g_hi.astype(F32)).astype(BF16)
    gc2 = _dot(tril16, jnp.concatenate([g_hi, g_lo], axis=1))
    gc = gc2[:, :head_dim] + gc2[:, head_dim:]
    gc_row = gc.T[0:1, :]
    gc_last = gc[tb - 1:tb, :]
    diff = jnp.concatenate([gc] * (tb // head_dim), axis=1) - gc_row
    decay = jnp.exp(diff + bias_ref[...])

    kb = k * beta
    qk = _dot_nt(jnp.concatenate([q, kb], axis=0).astype(BF16), k.astype(BF16))
    e_gc = jnp.exp(gc)
    return dict(
        attn16=(qk[:tb] * decay).astype(BF16),
        low16=(qk[tb:] * decay).astype(BF16),
        wu_rhs16=jnp.concatenate([kb * e_gc, v * beta], axis=1).astype(BF16),
        qg16=(q * e_gc).astype(BF16),
        kg16=(k * jnp.exp(gc_last - gc)).astype(BF16),
        g_last=jnp.exp(gc_last))


def _gdn_kernel(alog_ref, dtb_ref, q_ref, k_ref, v_ref, gate_ref, cwq_ref, cwk_ref, cwv_ref,
                ab_ref, og_ref, masks_ref, bias_ref, o_ref, state_ref, tail_ref,
                attn_ref, low_ref, wu_rhs_ref, qg_ref, kg_ref, glast_ref,
                *, heads_per_step, head_dim):
    hblk = pl.program_id(1)
    step = pl.program_id(2)
    tb = q_ref.shape[0]
    heads = range(heads_per_step)
    handover = dict(attn16=attn_ref, low16=low_ref, wu_rhs16=wu_rhs_ref, qg16=qg_ref,
                    kg16=kg_ref, g_last=glast_ref)

    @pl.when(step == 0)
    def _():
        state_ref[...] = jnp.zeros_like(state_ref)
        tail_ref[...] = jnp.zeros_like(tail_ref)
        for ref in handover.values():
            ref[...] = jnp.zeros_like(ref)

    def run(read, write):
        to_prepare = list(heads)

        def prepare_one():
            if to_prepare:
                hh = to_prepare.pop(0)
                prepared = _gdn_head_inputs(hh, hblk * heads_per_step + hh, alog_ref, dtb_ref,
                                            q_ref, k_ref, v_ref, cwq_ref, cwk_ref, cwv_ref,
                                            ab_ref, masks_ref, bias_ref, tail_ref, head_dim)
                for name, ref in handover.items():
                    ref[write, hh] = prepared[name]

        hd = [{name: ref[read, hh] for name, ref in handover.items()} for hh in heads]
        t_mats = _inv_unit_lower([h["low16"] for h in hd], masks_ref, prepare_one)
        wus = [_dot(t, h["wu_rhs16"]) for t, h in zip(t_mats, hd)]
        prepare_one()
        states = [state_ref[hh] for hh in heads]
        ws_qs = [_dot(jnp.concatenate([wu[:, :head_dim].astype(BF16), h["qg16"]], axis=0),
                      s.astype(BF16)) for wu, h, s in zip(wus, hd, states)]
        v_new16 = [(wu[:, head_dim:] - wq[:tb]).astype(BF16) for wu, wq in zip(wus, ws_qs)]
        outs = [wq[tb:] + _dot(h["attn16"], vn) for wq, h, vn in zip(ws_qs, hd, v_new16)]
        for hh in heads:
            state_ref[hh] = states[hh] * hd[hh]["g_last"] + _dot_tn(hd[hh]["kg16"], v_new16[hh])
        for hh in heads:
            cols = slice(hh * head_dim, (hh + 1) * head_dim)
            gate = gate_ref[:, cols].astype(F32)
            o = outs[hh]
            o = o * _rms_scale(o) * og_ref[...] * (gate * jax.nn.sigmoid(gate))
            o_ref[:, cols] = o.astype(o_ref.dtype)
        while to_prepare:
            prepare_one()

    @pl.when((step & 1) == 0)
    def _():
        run(1, 0)

    @pl.when((step & 1) == 1)
    def _():
        run(0, 1)


def gdn(proj, conv_w, ab, a_log, dt_bias, out_gain, *, n_heads, head_dim, heads_per_step=8):
    b, t, _ = proj.shape
    tb = min(GDN_BLOCK, t)
    assert t % tb == 0 and tb % head_dim == 0 and n_heads % heads_per_step == 0
    hb = heads_per_step
    n_hblk = n_heads // hb
    wblk = hb * head_dim
    masks_np = _gdn_masks(tb)
    masks = jnp.asarray(masks_np, dtype=BF16)
    bias = jnp.asarray(np.where(masks_np[MASK_INCL] > 0, 0.0, -np.inf), dtype=F32)
    vmem = masks.size * 2 + bias.size * 4 + 2 * 5 * tb * wblk * 2 + 40 * hb * tb * tb * 4

    n_tb = t // tb

    def prep_blk(ti):
        return jnp.minimum(ti, n_tb - 1)

    def solve_blk(ti):
        return jnp.maximum(ti - 1, 0)

    def act_spec(section, which):
        return pl.BlockSpec((None, tb, wblk),
                            lambda bi, hi, ti: (bi, which(ti), section * n_hblk + hi))

    def cw_spec(section):
        return pl.BlockSpec((CONV_TAPS, wblk), lambda bi, hi, ti: (0, section * n_hblk + hi))

    smem = pl.BlockSpec(memory_space=pltpu.SMEM)
    return pl.pallas_call(
        functools.partial(_gdn_kernel, heads_per_step=hb, head_dim=head_dim),
        grid=(b, n_hblk, n_tb + 1),
        in_specs=[smem, smem,
                  act_spec(0, prep_blk), act_spec(1, prep_blk), act_spec(2, prep_blk),
                  act_spec(3, solve_blk),
                  cw_spec(0), cw_spec(1), cw_spec(2),
                  pl.BlockSpec((None, hb, 2, tb), lambda bi, hi, ti: (bi, hi, 0, prep_blk(ti))),
                  pl.BlockSpec((1, head_dim), lambda bi, hi, ti: (0, 0)),
                  _resident(masks.shape, lambda bi, hi, ti: (0, 0, 0)),
                  _resident(bias.shape, lambda bi, hi, ti: (0, 0))],
        out_specs=pl.BlockSpec((None, tb, wblk), lambda bi, hi, ti: (bi, solve_blk(ti), hi)),
        out_shape=jax.ShapeDtypeStruct((b, t, n_heads * head_dim), BF16),
        scratch_shapes=[pltpu.VMEM((hb, head_dim, head_dim), F32),
                        pltpu.VMEM((3, hb, V7X_SUBLANES + tb, head_dim), F32),
                        pltpu.VMEM((2, hb, tb, tb), BF16),
                        pltpu.VMEM((2, hb, tb, tb), BF16),
                        pltpu.VMEM((2, hb, tb, 2 * head_dim), BF16),
                        pltpu.VMEM((2, hb, tb, head_dim), BF16),
                        pltpu.VMEM((2, hb, tb, head_dim), BF16),
                        pltpu.VMEM((2, hb, 1, head_dim), F32)],
        compiler_params=pltpu.CompilerParams(
            dimension_semantics=("arbitrary", "arbitrary", "arbitrary"),
            vmem_limit_bytes=_vmem_limit(vmem)),
        name="gdn",
    )(a_log.astype(F32), dt_bias.astype(F32), proj, proj, proj, proj,
      conv_w, conv_w, conv_w, ab, out_gain.reshape(1, head_dim).astype(F32), masks, bias)


def _sb_kernel(q_ref, k_ref, v_ref, o_ref, l1m_ref, lb_ref, rs_ref, a_ref, csum_ref, acc_ref,
               *, block, heads_per_step, head_dim):
    i = pl.program_id(2)
    ri = lax.broadcasted_iota(jnp.int32, (block, block), 0)
    ci = lax.broadcasted_iota(jnp.int32, (block, block), 1)
    upper = (ri > ci).astype(BF16)
    before = ci < ri
    heads = [slice(hh * head_dim, (hh + 1) * head_dim) for hh in range(heads_per_step)]

    def rows(blk):
        return pl.ds(pl.multiple_of(blk * block, block), block)

    def stage_a(blk, mask):
        def run(hh):
            cols = heads[hh]
            z = _dot_nt(q_ref[:, cols], k_ref[rows(blk), cols])
            sp = jnp.log2(1.0 + jnp.exp2(-jnp.abs(z)))
            log_beta = jnp.minimum(z, 0.0) - sp
            log_1m = log_beta - z
            if mask is not None:
                log_1m = jnp.where(mask, log_1m, 0.0)
                log_beta = jnp.where(mask, log_beta, -jnp.inf)
            l1m_ref[hh] = log_1m.astype(BF16)
            lb_ref[hh] = log_beta
            rs_ref[hh] = jnp.broadcast_to(jnp.sum(log_1m, axis=-1, keepdims=True),
                                          rs_ref.shape[1:])
        return run

    def stage_b(hh):
        csum = csum_ref[hh]
        tail = (_dot(l1m_ref[hh], upper)
                + jnp.concatenate([csum] * (block // V7X_LANES), axis=1))
        a_ref[hh] = jnp.exp2(lb_ref[hh] + tail).astype(BF16)
        csum_ref[hh] = csum + rs_ref[hh]

    def stage_c(blk):
        def run(hh):
            acc_ref[hh] += _dot(a_ref[hh], v_ref[rows(blk), heads[hh]])
        return run

    def region(*stages):
        for hh in range(heads_per_step):
            for stage in stages:
                stage(hh)

    acc_ref[...] = jnp.zeros_like(acc_ref)
    csum_ref[...] = jnp.zeros_like(csum_ref)
    region(stage_a(i, before))

    @pl.when(i >= 1)
    def _():
        region(stage_b, stage_a(i - 1, None))

    def body(t, carry):
        blk = i - t
        region(stage_c(blk + 2), stage_b, stage_a(blk, None))
        return carry

    lax.fori_loop(2, i + 1, body, 0)

    @pl.when(i >= 1)
    def _():
        region(stage_c(1), stage_b)

    @pl.when(i == 0)
    def _():
        region(stage_b)

    region(stage_c(0))
    for hh, cols in enumerate(heads):
        o_ref[:, cols] = acc_ref[hh].astype(o_ref.dtype)


def sb_attn(q, kv, *, n_heads, head_dim, block=256, heads_per_step=8):
    b, t, _ = q.shape
    blk = min(block, t)
    hp = heads_per_step
    assert t % blk == 0 and n_heads % hp == 0 and head_dim == V7X_LANES
    n_hblk = n_heads // hp
    wblk = hp * head_dim
    tile32 = hp * blk * blk * 4
    scratch_bytes = 2 * tile32 + 3 * hp * blk * V7X_LANES * 4
    vmem = 2 * t * wblk * 2 + 4 * blk * wblk * 2 + scratch_bytes + 4 * tile32

    def seq_spec(index_map):
        return pl.BlockSpec((None, t, wblk), index_map, pipeline_mode=pl.Buffered(1))

    return pl.pallas_call(
        functools.partial(_sb_kernel, block=blk, heads_per_step=hp, head_dim=head_dim),
        grid=(b, n_hblk, t // blk),
        in_specs=[pl.BlockSpec((None, blk, wblk), lambda bi, hi, qi: (bi, qi, hi)),
                  seq_spec(lambda bi, hi, qi: (bi, 0, hi)),
                  seq_spec(lambda bi, hi, qi: (bi, 0, n_hblk + hi))],
        out_specs=pl.BlockSpec((None, blk, wblk), lambda bi, hi, qi: (bi, qi, hi)),
        out_shape=jax.ShapeDtypeStruct((b, t, n_heads * head_dim), BF16),
        scratch_shapes=[pltpu.VMEM((hp, blk, blk), BF16),
                        pltpu.VMEM((hp, blk, blk), F32),
                        pltpu.VMEM((hp, blk, V7X_LANES), F32),
                        pltpu.VMEM((hp, blk, blk), BF16),
                        pltpu.VMEM((hp, blk, V7X_LANES), F32),
                        pltpu.VMEM((hp, blk, head_dim), F32)],
        compiler_params=pltpu.CompilerParams(
            dimension_semantics=("arbitrary", "arbitrary", "arbitrary"),
            vmem_limit_bytes=_vmem_limit(vmem)),
        name="sb_attn",
    )(q, kv, kv)


def kernel(x, mix_pre_gain, mix_post_gain, mlp_pre_gain, mlp_post_gain, mlp_w_up, mlp_w_down,
           gdn_w_in, gdn_conv_w, gdn_a_log, gdn_dt_bias, gdn_out_gain, gdn_w_out,
           kv_gain, w_kv, sb_w_q, sb_w_o):
    b, t, d = x.shape
    n = b * t
    n_a = gdn_w_in.shape[0]
    depth = mix_pre_gain.shape[0]
    gdn_heads = gdn_a_log.shape[1]
    gdn_dim = gdn_out_gain.shape[1]
    gdn_width = gdn_heads * gdn_dim
    sb_width = sb_w_q.shape[2]
    sb_dim = gdn_dim
    sb_heads = sb_width // sb_dim
    assert gdn_w_in.shape[2] == 4 * gdn_width + 2 * gdn_heads

    xs = x.reshape(n, d)
    kv = None
    for layer in range(depth):
        if layer < n_a:
            w_in = gdn_w_in[layer]
            w_main = w_in[:, :4 * gdn_width].astype(BF16)
            w_small = jnp.concatenate(
                [w_in[:, 4 * gdn_width + gdn_heads:], w_in[:, 4 * gdn_width:4 * gdn_width + gdn_heads],
                 jnp.zeros((d, V7X_LANES - 2 * gdn_heads), w_in.dtype)], axis=1).astype(BF16)
            pre = mix_pre_gain[layer]
            proj, small = norm_proj(xs, [(pre, w_main, BF16, 1.0), (pre, w_small, F32, 1.0)])
            small = small[:, :2 * gdn_heads].reshape(b, t, 2, gdn_heads)
            ab = jnp.transpose(small, (0, 3, 2, 1))
            o = gdn(proj.reshape(b, t, 4 * gdn_width), gdn_conv_w[layer], ab,
                    gdn_a_log[layer], gdn_dt_bias[layer], gdn_out_gain[layer],
                    n_heads=gdn_heads, head_dim=gdn_dim)
            w_o = gdn_w_out[layer]
        else:
            bl = layer - n_a
            q_scale = sb_dim ** -0.5 * math.log2(math.e)
            groups = [(mix_pre_gain[layer], sb_w_q[bl].astype(BF16), BF16, q_scale)]
            if kv is None:
                groups.append((kv_gain, w_kv.astype(BF16), BF16, 1.0))
                q, kv = norm_proj(xs, groups)
                kv = kv.reshape(b, t, 2 * sb_width)
            else:
                (q,) = norm_proj(xs, groups)
            o = sb_attn(q.reshape(b, t, sb_width), kv, n_heads=sb_heads, head_dim=sb_dim)
            w_o = sb_w_o[bl]
        xs = post_mlp(o.reshape(n, w_o.shape[0]), w_o.astype(BF16), mix_post_gain[layer], xs,
                      mlp_pre_gain[layer], mlp_w_up[layer].astype(BF16),
                      mlp_w_down[layer].astype(BF16), mlp_post_gain[layer])
    return xs.reshape(b, t, d)
```

```python
import functools
import math

import jax
import jax.numpy as jnp
import numpy as np
from jax import lax
from jax.experimental import pallas as pl
from jax.experimental.pallas import tpu as pltpu

F32 = jnp.float32
BF16 = jnp.bfloat16
EPS = 1e-6

V7X_LANES = 128
V7X_SUBLANES = 8
V7X_VMEM_BYTES = 64 * 1024 * 1024
V7X_VMEM_REQUEST_CAP = V7X_VMEM_BYTES - 8 * 1024 * 1024

GDN_BLOCK = 256
GDN_BASE = 8
CONV_TAPS = 4


def _vmem_limit(nbytes):
    return int(min(max(nbytes, 16 * 1024 * 1024), V7X_VMEM_REQUEST_CAP))


def _resident(shape, index_map):
    return pl.BlockSpec(shape, index_map, pipeline_mode=pl.Buffered(1))


def _rms_scale(x):
    return lax.rsqrt(jnp.mean(x * x, axis=-1, keepdims=True) + EPS)


def _dot(a, b):
    return jnp.dot(a, b, preferred_element_type=F32)


def _dot_nt(a, b):
    return lax.dot_general(a, b, (((1,), (1,)), ((), ())), preferred_element_type=F32)


def _dot_tn(a, b):
    return lax.dot_general(a, b, (((0,), (0,)), ((), ())), preferred_element_type=F32)


def _norm_proj_kernel(*refs, n_groups, scales, col_chunk):
    x_ref = refs[0]
    gain_refs = refs[1:1 + n_groups]
    w_refs = refs[1 + n_groups:1 + 2 * n_groups]
    out_refs = refs[1 + 2 * n_groups:]
    x = x_ref[...]
    xn = x * _rms_scale(x)
    for g in range(n_groups):
        h = (xn * gain_refs[g][...]).astype(BF16)
        n_cols = w_refs[g].shape[1]
        step = min(col_chunk, n_cols)
        for c in range(0, n_cols, step):
            acc = _dot(h, w_refs[g][:, c:c + step])
            if scales[g] != 1.0:
                acc = acc * scales[g]
            out_refs[g][:, c:c + step] = acc.astype(out_refs[g].dtype)


def norm_proj(x, groups, *, row_tile=512, col_chunk=1024):
    n, d = x.shape
    tm = min(row_tile, n)
    assert n % tm == 0
    n_groups = len(groups)
    gains = [g[0].reshape(1, d).astype(F32) for g in groups]
    ws = [g[1] for g in groups]
    in_specs = [pl.BlockSpec((tm, d), lambda i: (i, 0))]
    in_specs += [_resident((1, d), lambda i: (0, 0)) for _ in groups]
    in_specs += [_resident(w.shape, lambda i: (0, 0)) for w in ws]
    out_specs = [pl.BlockSpec((tm, w.shape[1]), lambda i: (i, 0)) for w in ws]
    out_shape = [jax.ShapeDtypeStruct((n, w.shape[1]), g[2]) for w, g in zip(ws, groups)]
    vmem = 2 * tm * d * 4 + sum(w.size * 2 for w in ws)
    vmem += sum(2 * tm * w.shape[1] * jnp.dtype(g[2]).itemsize for w, g in zip(ws, groups))
    vmem += 4 * tm * max(d, col_chunk) * 4
    return pl.pallas_call(
        functools.partial(_norm_proj_kernel, n_groups=n_groups,
                          scales=tuple(float(g[3]) for g in groups), col_chunk=col_chunk),
        grid=(n // tm,),
        in_specs=in_specs,
        out_specs=out_specs,
        out_shape=out_shape,
        compiler_params=pltpu.CompilerParams(
            dimension_semantics=("arbitrary",), vmem_limit_bytes=_vmem_limit(vmem)),
        name="norm_proj",
    )(x, *gains, *ws)


def _post_mlp_kernel(a_ref, wo_ref, gmix_ref, r_ref, gpre_ref, wup_ref, wdn_ref, gpost_ref, o_ref,
                     *, ff_chunk):
    mix = _dot(a_ref[...], wo_ref[...])
    x = r_ref[...] + mix * _rms_scale(mix) * gmix_ref[...]
    h = (x * _rms_scale(x) * gpre_ref[...]).astype(BF16)
    d_ff = wup_ref.shape[1]
    y = None
    for c in range(0, d_ff, ff_chunk):
        u = jnp.maximum(_dot(h, wup_ref[:, c:c + ff_chunk]), 0.0)
        part = _dot((u * u).astype(BF16), wdn_ref[c:c + ff_chunk, :])
        y = part if y is None else y + part
    o_ref[...] = x + y * _rms_scale(y) * gpost_ref[...]


def post_mlp(a, w_o, g_mix, resid, g_pre, w_up, w_down, g_post, *, row_tile=512, ff_chunk=1024):
    n, k = a.shape
    d = w_o.shape[1]
    d_ff = w_up.shape[1]
    tm = min(row_tile, n)
    fc = min(ff_chunk, d_ff)
    assert n % tm == 0 and d_ff % fc == 0

    def row_spec(width):
        return pl.BlockSpec((tm, width), lambda i: (i, 0))

    def gain(g):
        return g.reshape(1, d).astype(F32)

    vmem = 2 * tm * k * 2 + 4 * tm * d * 4 + (k * d + 2 * d * d_ff) * 2
    vmem += 3 * tm * fc * 4 + 4 * tm * d * 4
    return pl.pallas_call(
        functools.partial(_post_mlp_kernel, ff_chunk=fc),
        grid=(n // tm,),
        in_specs=[row_spec(k),
                  _resident((k, d), lambda i: (0, 0)),
                  _resident((1, d), lambda i: (0, 0)),
                  row_spec(d),
                  _resident((1, d), lambda i: (0, 0)),
                  _resident((d, d_ff), lambda i: (0, 0)),
                  _resident((d_ff, d), lambda i: (0, 0)),
                  _resident((1, d), lambda i: (0, 0))],
        out_specs=row_spec(d),
        out_shape=jax.ShapeDtypeStruct((n, d), F32),
        compiler_params=pltpu.CompilerParams(
            dimension_semantics=("arbitrary",), vmem_limit_bytes=_vmem_limit(vmem)),
        name="post_mlp",
    )(a, w_o, gain(g_mix), resid, gain(g_pre), w_up, w_down, gain(g_post))


def _conv_silu(x_bf16, w, win_ref):
    n = x_bf16.shape[0]
    x = x_bf16.astype(F32)
    win_ref[V7X_SUBLANES:, :] = x
    acc = x * w[CONV_TAPS - 1:CONV_TAPS, :]
    for s in range(1, CONV_TAPS):
        acc = acc + win_ref[pl.ds(V7X_SUBLANES - s, n), :] * w[CONV_TAPS - 1 - s:CONV_TAPS - s, :]
    win_ref[:V7X_SUBLANES, :] = x[n - V7X_SUBLANES:, :]
    return acc * jax.nn.sigmoid(acc)


def _softplus(x):
    return jnp.maximum(x, 0.0) + jnp.log(1.0 + jnp.exp(-jnp.abs(x)))


def _col_bcast(row, width):
    return jnp.broadcast_to(row, (width, row.shape[1])).T


MASK_INCL, MASK_EYE, MASK_BASE, MASK_MERGE0 = 0, 1, 2, 3


def _gdn_masks(n):
    ri = np.arange(n)[:, None]
    ci = np.arange(n)[None, :]
    masks = [ri >= ci, ri == ci, (ri // GDN_BASE == ci // GDN_BASE) & (ri > ci)]
    size = 2 * GDN_BASE
    while size <= n:
        masks.append((ri // size == ci // size) & (ri // (size // 2) > ci // (size // 2)))
        size *= 2
    return np.stack(masks).astype(np.float32)


def _inv_unit_lower(lows, masks_ref):
    eye = masks_ref[MASK_EYE]
    powers = [low * masks_ref[MASK_BASE] for low in lows]
    invs = [eye - p for p in powers]
    span = 2
    while span < GDN_BASE:
        powers = [_dot(p, p).astype(BF16) for p in powers]
        invs = [_dot(inv, eye + p).astype(BF16) for inv, p in zip(invs, powers)]
        span *= 2
    for level in range(MASK_MERGE0, masks_ref.shape[0]):
        halves = [_dot(inv, low * masks_ref[level]).astype(BF16) for inv, low in zip(invs, lows)]
        invs = [_dot(eye - h, inv).astype(BF16) for h, inv in zip(halves, invs)]
    return invs


def _gdn_head_inputs(hh, head, alog_ref, dtb_ref, q_ref, k_ref, v_ref, cwq_ref, cwk_ref, cwv_ref,
                     ab_ref, masks_ref, bias_ref, tail_ref, head_dim):
    tb = q_ref.shape[0]
    cols = slice(hh * head_dim, (hh + 1) * head_dim)
    q = _conv_silu(q_ref[:, cols], cwq_ref[:, cols], tail_ref.at[0, hh])
    k = _conv_silu(k_ref[:, cols], cwk_ref[:, cols], tail_ref.at[1, hh])
    v = _conv_silu(v_ref[:, cols], cwv_ref[:, cols], tail_ref.at[2, hh])
    q = q * (lax.rsqrt(jnp.sum(q * q, axis=-1, keepdims=True) + EPS) * head_dim ** -0.5)
    k = k * lax.rsqrt(jnp.sum(k * k, axis=-1, keepdims=True) + EPS)

    ab = ab_ref[hh]
    a_log = jnp.zeros((1, tb), F32) + alog_ref[head]
    g_row = -jnp.exp(a_log) * _softplus(ab[0:1] + dtb_ref[head])
    beta_row = jax.nn.sigmoid(ab[1:2])
    g_col = _col_bcast(g_row, head_dim)
    beta = _col_bcast(beta_row, head_dim)
    tril16 = masks_ref[MASK_INCL]
    g_hi = g_col.astype(BF16)
    g_lo = (g_col - g_hi.astype(F32)).astype(BF16)
    gc2 = _dot(tril16, jnp.concatenate([g_hi, g_lo], axis=1))
    gc = gc2[:, :head_dim] + gc2[:, head_dim:]
    gc_row = gc.T[0:1, :]
    gc_last = gc[tb - 1:tb, :]
    diff = jnp.concatenate([gc] * (tb // head_dim), axis=1) - gc_row
    decay = jnp.exp(diff + bias_ref[...])

    kb = k * beta
    qk = _dot_nt(jnp.concatenate([q, kb], axis=0).astype(BF16), k.astype(BF16))
    e_gc = jnp.exp(gc)
    return dict(
        attn16=(qk[:tb] * decay).astype(BF16),
        low16=(qk[tb:] * decay).astype(BF16),
        wu_rhs16=jnp.concatenate([kb * e_gc, v * beta], axis=1).astype(BF16),
        qg16=(q * e_gc).astype(BF16),
        kg16=(k * jnp.exp(gc_last - gc)).astype(BF16),
        g_last=jnp.exp(gc_last))


def _gdn_kernel(alog_ref, dtb_ref, q_ref, k_ref, v_ref, gate_ref, cwq_ref, cwk_ref, cwv_ref,
                ab_ref, og_ref, masks_ref, bias_ref, o_ref, state_ref, tail_ref,
                *, heads_per_step, head_dim):
    hblk = pl.program_id(1)
    tb = q_ref.shape[0]
    heads = range(heads_per_step)

    @pl.when(pl.program_id(2) == 0)
    def _():
        state_ref[...] = jnp.zeros_like(state_ref)
        tail_ref[...] = jnp.zeros_like(tail_ref)

    hd = [_gdn_head_inputs(hh, hblk * heads_per_step + hh, alog_ref, dtb_ref, q_ref, k_ref, v_ref,
                           cwq_ref, cwk_ref, cwv_ref, ab_ref, masks_ref, bias_ref, tail_ref,
                           head_dim)
          for hh in heads]
    t_mats = _inv_unit_lower([h["low16"] for h in hd], masks_ref)
    wus = [_dot(t, h["wu_rhs16"]) for t, h in zip(t_mats, hd)]
    states = [state_ref[hh] for hh in heads]
    ws_qs = [_dot(jnp.concatenate([wu[:, :head_dim].astype(BF16), h["qg16"]], axis=0),
                  s.astype(BF16)) for wu, h, s in zip(wus, hd, states)]
    v_new16 = [(wu[:, head_dim:] - wq[:tb]).astype(BF16) for wu, wq in zip(wus, ws_qs)]
    outs = [wq[tb:] + _dot(h["attn16"], vn) for wq, h, vn in zip(ws_qs, hd, v_new16)]
    for hh in heads:
        state_ref[hh] = states[hh] * hd[hh]["g_last"] + _dot_tn(hd[hh]["kg16"], v_new16[hh])
    for hh in heads:
        cols = slice(hh * head_dim, (hh + 1) * head_dim)
        gate = gate_ref[:, cols].astype(F32)
        o = outs[hh]
        o = o * _rms_scale(o) * og_ref[...] * (gate * jax.nn.sigmoid(gate))
        o_ref[:, cols] = o.astype(o_ref.dtype)


def gdn(proj, conv_w, ab, a_log, dt_bias, out_gain, *, n_heads, head_dim, heads_per_step=8):
    b, t, _ = proj.shape
    tb = min(GDN_BLOCK, t)
    assert t % tb == 0 and tb % head_dim == 0 and n_heads % heads_per_step == 0
    hb = heads_per_step
    n_hblk = n_heads // hb
    wblk = hb * head_dim
    masks_np = _gdn_masks(tb)
    masks = jnp.asarray(masks_np, dtype=BF16)
    bias = jnp.asarray(np.where(masks_np[MASK_INCL] > 0, 0.0, -np.inf), dtype=F32)
    vmem = masks.size * 2 + bias.size * 4 + 2 * 5 * tb * wblk * 2 + 40 * hb * tb * tb * 4

    def act_spec(section):
        return pl.BlockSpec((None, tb, wblk), lambda bi, hi, ti: (bi, ti, section * n_hblk + hi))

    def cw_spec(section):
        return pl.BlockSpec((CONV_TAPS, wblk), lambda bi, hi, ti: (0, section * n_hblk + hi))

    smem = pl.BlockSpec(memory_space=pltpu.SMEM)
    return pl.pallas_call(
        functools.partial(_gdn_kernel, heads_per_step=hb, head_dim=head_dim),
        grid=(b, n_hblk, t // tb),
        in_specs=[smem, smem,
                  act_spec(0), act_spec(1), act_spec(2), act_spec(3),
                  cw_spec(0), cw_spec(1), cw_spec(2),
                  pl.BlockSpec((None, hb, 2, tb), lambda bi, hi, ti: (bi, hi, 0, ti)),
                  pl.BlockSpec((1, head_dim), lambda bi, hi, ti: (0, 0)),
                  _resident(masks.shape, lambda bi, hi, ti: (0, 0, 0)),
                  _resident(bias.shape, lambda bi, hi, ti: (0, 0))],
        out_specs=pl.BlockSpec((None, tb, wblk), lambda bi, hi, ti: (bi, ti, hi)),
        out_shape=jax.ShapeDtypeStruct((b, t, n_heads * head_dim), BF16),
        scratch_shapes=[pltpu.VMEM((hb, head_dim, head_dim), F32),
                        pltpu.VMEM((3, hb, V7X_SUBLANES + tb, head_dim), F32)],
        compiler_params=pltpu.CompilerParams(
            dimension_semantics=("arbitrary", "arbitrary", "arbitrary"),
            vmem_limit_bytes=_vmem_limit(vmem)),
        name="gdn",
    )(a_log.astype(F32), dt_bias.astype(F32), proj, proj, proj, proj,
      conv_w, conv_w, conv_w, ab, out_gain.reshape(1, head_dim).astype(F32), masks, bias)


def _sb_kernel(q_ref, k_ref, v_ref, o_ref, l1m_ref, lb_ref, rs_ref, a_ref, csum_ref, acc_ref,
               *, block, heads_per_step, head_dim):
    i = pl.program_id(2)
    ri = lax.broadcasted_iota(jnp.int32, (block, block), 0)
    ci = lax.broadcasted_iota(jnp.int32, (block, block), 1)
    upper = (ri > ci).astype(BF16)
    before = ci < ri
    heads = [slice(hh * head_dim, (hh + 1) * head_dim) for hh in range(heads_per_step)]

    def rows(blk):
        return pl.ds(pl.multiple_of(blk * block, block), block)

    def stage_a(blk, mask):
        for hh, cols in enumerate(heads):
            z = _dot_nt(q_ref[:, cols], k_ref[rows(blk), cols])
            sp = jnp.log2(1.0 + jnp.exp2(-jnp.abs(z)))
            log_beta = jnp.minimum(z, 0.0) - sp
            log_1m = log_beta - z
            if mask is not None:
                log_1m = jnp.where(mask, log_1m, 0.0)
                log_beta = jnp.where(mask, log_beta, -jnp.inf)
            l1m_ref[hh] = log_1m.astype(BF16)
            lb_ref[hh] = log_beta
            rs_ref[hh] = jnp.broadcast_to(jnp.sum(log_1m, axis=-1, keepdims=True),
                                          rs_ref.shape[1:])

    def stage_b():
        for hh in range(heads_per_step):
            csum = csum_ref[hh]
            tail = (_dot(l1m_ref[hh], upper)
                    + jnp.concatenate([csum] * (block // V7X_LANES), axis=1))
            a_ref[hh] = jnp.exp2(lb_ref[hh] + tail).astype(BF16)
            csum_ref[hh] = csum + rs_ref[hh]

    def stage_c(blk):
        for hh, cols in enumerate(heads):
            acc_ref[hh] += _dot(a_ref[hh], v_ref[rows(blk), cols])

    acc_ref[...] = jnp.zeros_like(acc_ref)
    csum_ref[...] = jnp.zeros_like(csum_ref)
    stage_a(i, before)

    @pl.when(i >= 1)
    def _():
        stage_b()
        stage_a(i - 1, None)

    def body(t, carry):
        blk = i - t
        stage_c(blk + 2)
        stage_b()
        stage_a(blk, None)
        return carry

    lax.fori_loop(2, i + 1, body, 0)

    @pl.when(i >= 1)
    def _():
        stage_c(1)
        stage_b()

    @pl.when(i == 0)
    def _():
        stage_b()

    stage_c(0)
    for hh, cols in enumerate(heads):
        o_ref[:, cols] = acc_ref[hh].astype(o_ref.dtype)


def sb_attn(q, kv, *, n_heads, head_dim, block=256, heads_per_step=8):
    b, t, _ = q.shape
    blk = min(block, t)
    hp = heads_per_step
    assert t % blk == 0 and n_heads % hp == 0 and head_dim == V7X_LANES
    n_hblk = n_heads // hp
    wblk = hp * head_dim
    tile32 = hp * blk * blk * 4
    scratch_bytes = 2 * tile32 + 3 * hp * blk * V7X_LANES * 4
    vmem = 2 * t * wblk * 2 + 4 * blk * wblk * 2 + scratch_bytes + 4 * tile32

    def seq_spec(index_map):
        return pl.BlockSpec((None, t, wblk), index_map, pipeline_mode=pl.Buffered(1))

    return pl.pallas_call(
        functools.partial(_sb_kernel, block=blk, heads_per_step=hp, head_dim=head_dim),
        grid=(b, n_hblk, t // blk),
        in_specs=[pl.BlockSpec((None, blk, wblk), lambda bi, hi, qi: (bi, qi, hi)),
                  seq_spec(lambda bi, hi, qi: (bi, 0, hi)),
                  seq_spec(lambda bi, hi, qi: (bi, 0, n_hblk + hi))],
        out_specs=pl.BlockSpec((None, blk, wblk), lambda bi, hi, qi: (bi, qi, hi)),
        out_shape=jax.ShapeDtypeStruct((b, t, n_heads * head_dim), BF16),
        scratch_shapes=[pltpu.VMEM((hp, blk, blk), BF16),
                        pltpu.VMEM((hp, blk, blk), F32),
                        pltpu.VMEM((hp, blk, V7X_LANES), F32),
                        pltpu.VMEM((hp, blk, blk), BF16),
                        pltpu.VMEM((hp, blk, V7X_LANES), F32),
                        pltpu.VMEM((hp, blk, head_dim), F32)],
        compiler_params=pltpu.CompilerParams(
            dimension_semantics=("arbitrary", "arbitrary", "arbitrary"),
            vmem_limit_bytes=_vmem_limit(vmem)),
        name="sb_attn",
    )(q, kv, kv)


def kernel(x, mix_pre_gain, mix_post_gain, mlp_pre_gain, mlp_post_gain, mlp_w_up, mlp_w_down,
           gdn_w_in, gdn_conv_w, gdn_a_log, gdn_dt_bias, gdn_out_gain, gdn_w_out,
           kv_gain, w_kv, sb_w_q, sb_w_o):
    b, t, d = x.shape
    n = b * t
    n_a = gdn_w_in.shape[0]
    depth = mix_pre_gain.shape[0]
    gdn_heads = gdn_a_log.shape[1]
    gdn_dim = gdn_out_gain.shape[1]
    gdn_width = gdn_heads * gdn_dim
    sb_width = sb_w_q.shape[2]
    sb_dim = gdn_dim
    sb_heads = sb_width // sb_dim
    assert gdn_w_in.shape[2] == 4 * gdn_width + 2 * gdn_heads

    xs = x.reshape(n, d)
    kv = None
    for layer in range(depth):
        if layer < n_a:
            w_in = gdn_w_in[layer]
            w_main = w_in[:, :4 * gdn_width].astype(BF16)
            w_small = jnp.concatenate(
                [w_in[:, 4 * gdn_width + gdn_heads:], w_in[:, 4 * gdn_width:4 * gdn_width + gdn_heads],
                 jnp.zeros((d, V7X_LANES - 2 * gdn_heads), w_in.dtype)], axis=1).astype(BF16)
            pre = mix_pre_gain[layer]
            proj, small = norm_proj(xs, [(pre, w_main, BF16, 1.0), (pre, w_small, F32, 1.0)])
            small = small[:, :2 * gdn_heads].reshape(b, t, 2, gdn_heads)
            ab = jnp.transpose(small, (0, 3, 2, 1))
            o = gdn(proj.reshape(b, t, 4 * gdn_width), gdn_conv_w[layer], ab,
                    gdn_a_log[layer], gdn_dt_bias[layer], gdn_out_gain[layer],
                    n_heads=gdn_heads, head_dim=gdn_dim)
            w_o = gdn_w_out[layer]
        else:
            bl = layer - n_a
            q_scale = sb_dim ** -0.5 * math.log2(math.e)
            groups = [(mix_pre_gain[layer], sb_w_q[bl].astype(BF16), BF16, q_scale)]
            if kv is None:
                groups.append((kv_gain, w_kv.astype(BF16), BF16, 1.0))
                q, kv = norm_proj(xs, groups)
                kv = kv.reshape(b, t, 2 * sb_width)
            else:
                (q,) = norm_proj(xs, groups)
            o = sb_attn(q.reshape(b, t, sb_width), kv, n_heads=sb_heads, head_dim=sb_dim)
            w_o = sb_w_o[bl]
        xs = post_mlp(o.reshape(n, w_o.shape[0]), w_o.astype(BF16), mix_post_gain[layer], xs,
                      mlp_pre_gain[layer], mlp_w_up[layer].astype(BF16),
                      mlp_w_down[layer].astype(BF16), mlp_post_gain[layer])
    return xs.reshape(b, t, d)
```

```python
import functools
import math

import jax
import jax.numpy as jnp
import numpy as np
from jax import lax
from jax.experimental import pallas as pl
from jax.experimental.pallas import tpu as pltpu

F32 = jnp.float32
BF16 = jnp.bfloat16
EPS = 1e-6

V7X_LANES = 128
V7X_SUBLANES = 8
V7X_VMEM_BYTES = 64 * 1024 * 1024
V7X_VMEM_REQUEST_CAP = V7X_VMEM_BYTES - 8 * 1024 * 1024

GDN_BLOCK = 256
GDN_BASE = 8
CONV_TAPS = 4


def _vmem_limit(nbytes):
    return int(min(max(nbytes, 16 * 1024 * 1024), V7X_VMEM_REQUEST_CAP))


def _resident(shape, index_map):
    return pl.BlockSpec(shape, index_map, pipeline_mode=pl.Buffered(1))


def _rms_scale(x):
    return lax.rsqrt(jnp.mean(x * x, axis=-1, keepdims=True) + EPS)


def _dot(a, b):
    return jnp.dot(a, b, preferred_element_type=F32)


def _dot_nt(a, b):
    return lax.dot_general(a, b, (((1,), (1,)), ((), ())), preferred_element_type=F32)


def _dot_tn(a, b):
    return lax.dot_general(a, b, (((0,), (0,)), ((), ())), preferred_element_type=F32)


def _norm_proj_kernel(*refs, n_groups, scales, col_chunk):
    x_ref = refs[0]
    gain_refs = refs[1:1 + n_groups]
    w_refs = refs[1 + n_groups:1 + 2 * n_groups]
    out_refs = refs[1 + 2 * n_groups:]
    x = x_ref[...]
    xn = x * _rms_scale(x)
    for g in range(n_groups):
        h = (xn * gain_refs[g][...]).astype(BF16)
        n_cols = w_refs[g].shape[1]
        step = min(col_chunk, n_cols)
        for c in range(0, n_cols, step):
            acc = _dot(h, w_refs[g][:, c:c + step])
            if scales[g] != 1.0:
                acc = acc * scales[g]
            out_refs[g][:, c:c + step] = acc.astype(out_refs[g].dtype)


def norm_proj(x, groups, *, row_tile=512, col_chunk=1024):
    n, d = x.shape
    tm = min(row_tile, n)
    assert n % tm == 0
    n_groups = len(groups)
    gains = [g[0].reshape(1, d).astype(F32) for g in groups]
    ws = [g[1] for g in groups]
    in_specs = [pl.BlockSpec((tm, d), lambda i: (i, 0))]
    in_specs += [_resident((1, d), lambda i: (0, 0)) for _ in groups]
    in_specs += [_resident(w.shape, lambda i: (0, 0)) for w in ws]
    out_specs = [pl.BlockSpec((tm, w.shape[1]), lambda i: (i, 0)) for w in ws]
    out_shape = [jax.ShapeDtypeStruct((n, w.shape[1]), g[2]) for w, g in zip(ws, groups)]
    vmem = 2 * tm * d * 4 + sum(w.size * 2 for w in ws)
    vmem += sum(2 * tm * w.shape[1] * jnp.dtype(g[2]).itemsize for w, g in zip(ws, groups))
    vmem += 4 * tm * max(d, col_chunk) * 4
    return pl.pallas_call(
        functools.partial(_norm_proj_kernel, n_groups=n_groups,
                          scales=tuple(float(g[3]) for g in groups), col_chunk=col_chunk),
        grid=(n // tm,),
        in_specs=in_specs,
        out_specs=out_specs,
        out_shape=out_shape,
        compiler_params=pltpu.CompilerParams(
            dimension_semantics=("arbitrary",), vmem_limit_bytes=_vmem_limit(vmem)),
        name="norm_proj",
    )(x, *gains, *ws)


def _post_mlp_kernel(a_ref, wo_ref, gmix_ref, r_ref, gpre_ref, wup_ref, wdn_ref, gpost_ref, o_ref,
                     *, ff_chunk):
    mix = _dot(a_ref[...], wo_ref[...])
    x = r_ref[...] + mix * _rms_scale(mix) * gmix_ref[...]
    h = (x * _rms_scale(x) * gpre_ref[...]).astype(BF16)
    d_ff = wup_ref.shape[1]
    y = None
    for c in range(0, d_ff, ff_chunk):
        u = jnp.maximum(_dot(h, wup_ref[:, c:c + ff_chunk]), 0.0)
        part = _dot((u * u).astype(BF16), wdn_ref[c:c + ff_chunk, :])
        y = part if y is None else y + part
    o_ref[...] = x + y * _rms_scale(y) * gpost_ref[...]


def post_mlp(a, w_o, g_mix, resid, g_pre, w_up, w_down, g_post, *, row_tile=512, ff_chunk=1024):
    n, k = a.shape
    d = w_o.shape[1]
    d_ff = w_up.shape[1]
    tm = min(row_tile, n)
    fc = min(ff_chunk, d_ff)
    assert n % tm == 0 and d_ff % fc == 0

    def row_spec(width):
        return pl.BlockSpec((tm, width), lambda i: (i, 0))

    def gain(g):
        return g.reshape(1, d).astype(F32)

    vmem = 2 * tm * k * 2 + 4 * tm * d * 4 + (k * d + 2 * d * d_ff) * 2
    vmem += 3 * tm * fc * 4 + 4 * tm * d * 4
    return pl.pallas_call(
        functools.partial(_post_mlp_kernel, ff_chunk=fc),
        grid=(n // tm,),
        in_specs=[row_spec(k),
                  _resident((k, d), lambda i: (0, 0)),
                  _resident((1, d), lambda i: (0, 0)),
                  row_spec(d),
                  _resident((1, d), lambda i: (0, 0)),
                  _resident((d, d_ff), lambda i: (0, 0)),
                  _resident((d_ff, d), lambda i: (0, 0)),
                  _resident((1, d), lambda i: (0, 0))],
        out_specs=row_spec(d),
        out_shape=jax.ShapeDtypeStruct((n, d), F32),
        compiler_params=pltpu.CompilerParams(
            dimension_semantics=("arbitrary",), vmem_limit_bytes=_vmem_limit(vmem)),
        name="post_mlp",
    )(a, w_o, gain(g_mix), resid, gain(g_pre), w_up, w_down, gain(g_post))


def _conv_silu(x_bf16, w, win_ref):
    n = x_bf16.shape[0]
    x = x_bf16.astype(F32)
    win_ref[V7X_SUBLANES:, :] = x
    acc = x * w[CONV_TAPS - 1:CONV_TAPS, :]
    for s in range(1, CONV_TAPS):
        acc = acc + win_ref[pl.ds(V7X_SUBLANES - s, n), :] * w[CONV_TAPS - 1 - s:CONV_TAPS - s, :]
    win_ref[:V7X_SUBLANES, :] = x[n - V7X_SUBLANES:, :]
    return acc * jax.nn.sigmoid(acc)


def _softplus(x):
    return jnp.maximum(x, 0.0) + jnp.log(1.0 + jnp.exp(-jnp.abs(x)))


def _col_bcast(row, width):
    return jnp.broadcast_to(row, (width, row.shape[1])).T


MASK_INCL, MASK_EYE, MASK_BASE, MASK_MERGE0 = 0, 1, 2, 3


def _gdn_masks(n):
    ri = np.arange(n)[:, None]
    ci = np.arange(n)[None, :]
    masks = [ri >= ci, ri == ci, (ri // GDN_BASE == ci // GDN_BASE) & (ri > ci)]
    size = 2 * GDN_BASE
    while size <= n:
        masks.append((ri // size == ci // size) & (ri // (size // 2) > ci // (size // 2)))
        size *= 2
    return np.stack(masks).astype(np.float32)


def _inv_unit_lower(lows, masks_ref):
    eye = masks_ref[MASK_EYE]
    powers = [low * masks_ref[MASK_BASE] for low in lows]
    invs = [eye - p for p in powers]
    span = 2
    while span < GDN_BASE:
        powers = [_dot(p, p).astype(BF16) for p in powers]
        invs = [_dot(inv, eye + p).astype(BF16) for inv, p in zip(invs, powers)]
        span *= 2
    for level in range(MASK_MERGE0, masks_ref.shape[0]):
        halves = [_dot(inv, low * masks_ref[level]).astype(BF16) for inv, low in zip(invs, lows)]
        invs = [_dot(eye - h, inv).astype(BF16) for h, inv in zip(halves, invs)]
    return invs


def _gdn_head_inputs(hh, head, alog_ref, dtb_ref, q_ref, k_ref, v_ref, cwq_ref, cwk_ref, cwv_ref,
                     ab_ref, masks_ref, bias_ref, tail_ref, head_dim):
    tb = q_ref.shape[0]
    cols = slice(hh * head_dim, (hh + 1) * head_dim)
    q = _conv_silu(q_ref[:, cols], cwq_ref[:, cols], tail_ref.at[0, hh])
    k = _conv_silu(k_ref[:, cols], cwk_ref[:, cols], tail_ref.at[1, hh])
    v = _conv_silu(v_ref[:, cols], cwv_ref[:, cols], tail_ref.at[2, hh])
    q = q * (lax.rsqrt(jnp.sum(q * q, axis=-1, keepdims=True) + EPS) * head_dim ** -0.5)
    k = k * lax.rsqrt(jnp.sum(k * k, axis=-1, keepdims=True) + EPS)

    ab = ab_ref[hh]
    a_log = jnp.zeros((1, tb), F32) + alog_ref[head]
    g_row = -jnp.exp(a_log) * _softplus(ab[0:1] + dtb_ref[head])
    beta_row = jax.nn.sigmoid(ab[1:2])
    g_col = _col_bcast(g_row, head_dim)
    beta = _col_bcast(beta_row, head_dim)
    tril16 = masks_ref[MASK_INCL]
    g_hi = g_col.astype(BF16)
    g_lo = (g_col - g_hi.astype(F32)).astype(BF16)
    gc2 = _dot(tril16, jnp.concatenate([g_hi, g_lo], axis=1))
    gc = gc2[:, :head_dim] + gc2[:, head_dim:]
    gc_row = gc.T[0:1, :]
    gc_last = gc[tb - 1:tb, :]
    diff = jnp.concatenate([gc] * (tb // head_dim), axis=1) - gc_row
    decay = jnp.exp(diff + bias_ref[...])

    kb = k * beta
    qk = _dot_nt(jnp.concatenate([q, kb], axis=0).astype(BF16), k.astype(BF16))
    e_gc = jnp.exp(gc)
    return dict(
        attn16=(qk[:tb] * decay).astype(BF16),
        low16=(qk[tb:] * decay).astype(BF16),
        wu_rhs16=jnp.concatenate([kb * e_gc, v * beta], axis=1).astype(BF16),
        qg16=(q * e_gc).astype(BF16),
        kg16=(k * jnp.exp(gc_last - gc)).astype(BF16),
        g_last=jnp.exp(gc_last))


def _gdn_kernel(alog_ref, dtb_ref, q_ref, k_ref, v_ref, gate_ref, cwq_ref, cwk_ref, cwv_ref,
                ab_ref, og_ref, masks_ref, bias_ref, o_ref, state_ref, tail_ref,
                *, heads_per_step, head_dim):
    hblk = pl.program_id(1)
    tb = q_ref.shape[0]
    heads = range(heads_per_step)

    @pl.when(pl.program_id(2) == 0)
    def _():
        state_ref[...] = jnp.zeros_like(state_ref)
        tail_ref[...] = jnp.zeros_like(tail_ref)

    hd = [_gdn_head_inputs(hh, hblk * heads_per_step + hh, alog_ref, dtb_ref, q_ref, k_ref, v_ref,
                           cwq_ref, cwk_ref, cwv_ref, ab_ref, masks_ref, bias_ref, tail_ref,
                           head_dim)
          for hh in heads]
    t_mats = _inv_unit_lower([h["low16"] for h in hd], masks_ref)
    wus = [_dot(t, h["wu_rhs16"]) for t, h in zip(t_mats, hd)]
    states = [state_ref[hh] for hh in heads]
    ws_qs = [_dot(jnp.concatenate([wu[:, :head_dim].astype(BF16), h["qg16"]], axis=0),
                  s.astype(BF16)) for wu, h, s in zip(wus, hd, states)]
    v_new16 = [(wu[:, head_dim:] - wq[:tb]).astype(BF16) for wu, wq in zip(wus, ws_qs)]
    outs = [wq[tb:] + _dot(h["attn16"], vn) for wq, h, vn in zip(ws_qs, hd, v_new16)]
    for hh in heads:
        state_ref[hh] = states[hh] * hd[hh]["g_last"] + _dot_tn(hd[hh]["kg16"], v_new16[hh])
    for hh in heads:
        cols = slice(hh * head_dim, (hh + 1) * head_dim)
        gate = gate_ref[:, cols].astype(F32)
        o = outs[hh]
        o = o * _rms_scale(o) * og_ref[...] * (gate * jax.nn.sigmoid(gate))
        o_ref[:, cols] = o.astype(o_ref.dtype)


def gdn(proj, conv_w, ab, a_log, dt_bias, out_gain, *, n_heads, head_dim, heads_per_step=8):
    b, t, _ = proj.shape
    tb = min(GDN_BLOCK, t)
    assert t % tb == 0 and tb % head_dim == 0 and n_heads % heads_per_step == 0
    hb = heads_per_step
    n_hblk = n_heads // hb
    wblk = hb * head_dim
    masks_np = _gdn_masks(tb)
    masks = jnp.asarray(masks_np, dtype=BF16)
    bias = jnp.asarray(np.where(masks_np[MASK_INCL] > 0, 0.0, -np.inf), dtype=F32)
    vmem = masks.size * 2 + bias.size * 4 + 2 * 5 * tb * wblk * 2 + 40 * hb * tb * tb * 4

    def act_spec(section):
        return pl.BlockSpec((None, tb, wblk), lambda bi, hi, ti: (bi, ti, section * n_hblk + hi))

    def cw_spec(section):
        return pl.BlockSpec((CONV_TAPS, wblk), lambda bi, hi, ti: (0, section * n_hblk + hi))

    smem = pl.BlockSpec(memory_space=pltpu.SMEM)
    return pl.pallas_call(
        functools.partial(_gdn_kernel, heads_per_step=hb, head_dim=head_dim),
        grid=(b, n_hblk, t // tb),
        in_specs=[smem, smem,
                  act_spec(0), act_spec(1), act_spec(2), act_spec(3),
                  cw_spec(0), cw_spec(1), cw_spec(2),
                  pl.BlockSpec((None, hb, 2, tb), lambda bi, hi, ti: (bi, hi, 0, ti)),
                  pl.BlockSpec((1, head_dim), lambda bi, hi, ti: (0, 0)),
                  _resident(masks.shape, lambda bi, hi, ti: (0, 0, 0)),
                  _resident(bias.shape, lambda bi, hi, ti: (0, 0))],
        out_specs=pl.BlockSpec((None, tb, wblk), lambda bi, hi, ti: (bi, ti, hi)),
        out_shape=jax.ShapeDtypeStruct((b, t, n_heads * head_dim), BF16),
        scratch_shapes=[pltpu.VMEM((hb, head_dim, head_dim), F32),
                        pltpu.VMEM((3, hb, V7X_SUBLANES + tb, head_dim), F32)],
        compiler_params=pltpu.CompilerParams(
            dimension_semantics=("arbitrary", "arbitrary", "arbitrary"),
            vmem_limit_bytes=_vmem_limit(vmem)),
        name="gdn",
    )(a_log.astype(F32), dt_bias.astype(F32), proj, proj, proj, proj,
      conv_w, conv_w, conv_w, ab, out_gain.reshape(1, head_dim).astype(F32), masks, bias)


def _sb_kernel(q_ref, k_ref, v_ref, o_ref, l1m_ref, lb_ref, rs_ref, a_ref, csum_ref, acc_ref,
               *, block, heads_per_step, head_dim):
    i = pl.program_id(2)
    ri = lax.broadcasted_iota(jnp.int32, (block, block), 0)
    ci = lax.broadcasted_iota(jnp.int32, (block, block), 1)
    upper = (ri > ci).astype(BF16)
    before = ci < ri
    heads = [slice(hh * head_dim, (hh + 1) * head_dim) for hh in range(heads_per_step)]

    def rows(blk):
        return pl.ds(pl.multiple_of(blk * block, block), block)

    def stage_a(blk, mask):
        for hh, cols in enumerate(heads):
            z = _dot_nt(q_ref[:, cols], k_ref[rows(blk), cols])
            sp = jnp.log2(1.0 + jnp.exp2(-jnp.abs(z)))
            log_beta = jnp.minimum(z, 0.0) - sp
            log_1m = log_beta - z
            if mask is not None:
                log_1m = jnp.where(mask, log_1m, 0.0)
                log_beta = jnp.where(mask, log_beta, -jnp.inf)
            l1m_ref[hh] = log_1m.astype(BF16)
            lb_ref[hh] = log_beta
            rs_ref[hh] = jnp.broadcast_to(jnp.sum(log_1m, axis=-1, keepdims=True),
                                          rs_ref.shape[1:])

    def stage_b():
        for hh in range(heads_per_step):
            csum = csum_ref[hh]
            tail = (_dot(l1m_ref[hh], upper)
                    + jnp.concatenate([csum] * (block // V7X_LANES), axis=1))
            a_ref[hh] = jnp.exp2(lb_ref[hh] + tail).astype(BF16)
            csum_ref[hh] = csum + rs_ref[hh]

    def stage_c(blk):
        for hh, cols in enumerate(heads):
            acc_ref[hh] += _dot(a_ref[hh], v_ref[rows(blk), cols])

    acc_ref[...] = jnp.zeros_like(acc_ref)
    csum_ref[...] = jnp.zeros_like(csum_ref)

    @pl.when(i == 0)
    def _():
        stage_a(0, before)
        stage_b()
        stage_c(0)

    @pl.when(i >= 1)
    def _():
        stage_a(i, before)
        stage_b()
        stage_a(i - 1, None)

    def body(t, carry):
        blk = i - t
        stage_c(blk + 2)
        stage_b()
        stage_a(blk, None)
        return carry

    lax.fori_loop(2, i + 1, body, 0)

    @pl.when(i >= 1)
    def _():
        stage_c(1)
        stage_b()
        stage_c(0)

    for hh, cols in enumerate(heads):
        o_ref[:, cols] = acc_ref[hh].astype(o_ref.dtype)


def sb_attn(q, kv, *, n_heads, head_dim, block=256, heads_per_step=8):
    b, t, _ = q.shape
    blk = min(block, t)
    hp = heads_per_step
    assert t % blk == 0 and n_heads % hp == 0 and head_dim == V7X_LANES
    n_hblk = n_heads // hp
    wblk = hp * head_dim
    tile32 = hp * blk * blk * 4
    scratch_bytes = 2 * tile32 + 3 * hp * blk * V7X_LANES * 4
    vmem = 2 * t * wblk * 2 + 4 * blk * wblk * 2 + scratch_bytes + 4 * tile32

    def seq_spec(index_map):
        return pl.BlockSpec((None, t, wblk), index_map, pipeline_mode=pl.Buffered(1))

    return pl.pallas_call(
        functools.partial(_sb_kernel, block=blk, heads_per_step=hp, head_dim=head_dim),
        grid=(b, n_hblk, t // blk),
        in_specs=[pl.BlockSpec((None, blk, wblk), lambda bi, hi, qi: (bi, qi, hi)),
                  seq_spec(lambda bi, hi, qi: (bi, 0, hi)),
                  seq_spec(lambda bi, hi, qi: (bi, 0, n_hblk + hi))],
        out_specs=pl.BlockSpec((None, blk, wblk), lambda bi, hi, qi: (bi, qi, hi)),
        out_shape=jax.ShapeDtypeStruct((b, t, n_heads * head_dim), BF16),
        scratch_shapes=[pltpu.VMEM((hp, blk, blk), BF16),
                        pltpu.VMEM((hp, blk, blk), F32),
                        pltpu.VMEM((hp, blk, V7X_LANES), F32),
                        pltpu.VMEM((hp, blk, blk), BF16),
                        pltpu.VMEM((hp, blk, V7X_LANES), F32),
                        pltpu.VMEM((hp, blk, head_dim), F32)],
        compiler_params=pltpu.CompilerParams(
            dimension_semantics=("arbitrary", "arbitrary", "arbitrary"),
            vmem_limit_bytes=_vmem_limit(vmem)),
        name="sb_attn",
    )(q, kv, kv)


def kernel(x, mix_pre_gain, mix_post_gain, mlp_pre_gain, mlp_post_gain, mlp_w_up, mlp_w_down,
           gdn_w_in, gdn_conv_w, gdn_a_log, gdn_dt_bias, gdn_out_gain, gdn_w_out,
           kv_gain, w_kv, sb_w_q, sb_w_o):
    b, t, d = x.shape
    n = b * t
    n_a = gdn_w_in.shape[0]
    depth = mix_pre_gain.shape[0]
    gdn_heads = gdn_a_log.shape[1]
    gdn_dim = gdn_out_gain.shape[1]
    gdn_width = gdn_heads * gdn_dim
    sb_width = sb_w_q.shape[2]
    sb_dim = gdn_dim
    sb_heads = sb_width // sb_dim
    assert gdn_w_in.shape[2] == 4 * gdn_width + 2 * gdn_heads

    xs = x.reshape(n, d)
    kv = None
    for layer in range(depth):
        if layer < n_a:
            w_in = gdn_w_in[layer]
            w_main = w_in[:, :4 * gdn_width].astype(BF16)
            w_small = jnp.concatenate(
                [w_in[:, 4 * gdn_width + gdn_heads:], w_in[:, 4 * gdn_width:4 * gdn_width + gdn_heads],
                 jnp.zeros((d, V7X_LANES - 2 * gdn_heads), w_in.dtype)], axis=1).astype(BF16)
            pre = mix_pre_gain[layer]
            proj, small = norm_proj(xs, [(pre, w_main, BF16, 1.0), (pre, w_small, F32, 1.0)])
            small = small[:, :2 * gdn_heads].reshape(b, t, 2, gdn_heads)
            ab = jnp.transpose(small, (0, 3, 2, 1))
            o = gdn(proj.reshape(b, t, 4 * gdn_width), gdn_conv_w[layer], ab,
                    gdn_a_log[layer], gdn_dt_bias[layer], gdn_out_gain[layer],
                    n_heads=gdn_heads, head_dim=gdn_dim)
            w_o = gdn_w_out[layer]
        else:
            bl = layer - n_a
            q_scale = sb_dim ** -0.5 * math.log2(math.e)
            groups = [(mix_pre_gain[layer], sb_w_q[bl].astype(BF16), BF16, q_scale)]
            if kv is None:
                groups.append((kv_gain, w_kv.astype(BF16), BF16, 1.0))
                q, kv = norm_proj(xs, groups)
                kv = kv.reshape(b, t, 2 * sb_width)
            else:
                (q,) = norm_proj(xs, groups)
            o = sb_attn(q.reshape(b, t, sb_width), kv, n_heads=sb_heads, head_dim=sb_dim)
            w_o = sb_w_o[bl]
        xs = post_mlp(o.reshape(n, w_o.shape[0]), w_o.astype(BF16), mix_post_gain[layer], xs,
                      mlp_pre_gain[layer], mlp_w_up[layer].astype(BF16),
                      mlp_w_down[layer].astype(BF16), mlp_post_gain[layer])
    return xs.reshape(b, t, d)
```

```python
import functools
import math

import jax
import jax.numpy as jnp
import numpy as np
from jax import lax
from jax.experimental import pallas as pl
from jax.experimental.pallas import tpu as pltpu

F32 = jnp.float32
BF16 = jnp.bfloat16
EPS = 1e-6

V7X_LANES = 128
V7X_SUBLANES = 8
V7X_VMEM_BYTES = 64 * 1024 * 1024
V7X_VMEM_REQUEST_CAP = V7X_VMEM_BYTES - 8 * 1024 * 1024

GDN_BLOCK = 256
GDN_BASE = 8
CONV_TAPS = 4


def _vmem_limit(nbytes):
    return int(min(max(nbytes, 16 * 1024 * 1024), V7X_VMEM_REQUEST_CAP))


def _resident(shape, index_map):
    return pl.BlockSpec(shape, index_map, pipeline_mode=pl.Buffered(1))


def _rms_scale(x):
    return lax.rsqrt(jnp.mean(x * x, axis=-1, keepdims=True) + EPS)


def _dot(a, b):
    return jnp.dot(a, b, preferred_element_type=F32)


def _dot_nt(a, b):
    return lax.dot_general(a, b, (((1,), (1,)), ((), ())), preferred_element_type=F32)


def _dot_tn(a, b):
    return lax.dot_general(a, b, (((0,), (0,)), ((), ())), preferred_element_type=F32)


def _norm_proj_kernel(*refs, n_groups, scales, col_chunk):
    x_ref = refs[0]
    gain_refs = refs[1:1 + n_groups]
    w_refs = refs[1 + n_groups:1 + 2 * n_groups]
    out_refs = refs[1 + 2 * n_groups:]
    x = x_ref[...]
    xn = x * _rms_scale(x)
    for g in range(n_groups):
        h = (xn * gain_refs[g][...]).astype(BF16)
        n_cols = w_refs[g].shape[1]
        step = min(col_chunk, n_cols)
        for c in range(0, n_cols, step):
            acc = _dot(h, w_refs[g][:, c:c + step])
            if scales[g] != 1.0:
                acc = acc * scales[g]
            out_refs[g][:, c:c + step] = acc.astype(out_refs[g].dtype)


def norm_proj(x, groups, *, row_tile=512, col_chunk=1024):
    n, d = x.shape
    tm = min(row_tile, n)
    assert n % tm == 0
    n_groups = len(groups)
    gains = [g[0].reshape(1, d).astype(F32) for g in groups]
    ws = [g[1] for g in groups]
    in_specs = [pl.BlockSpec((tm, d), lambda i: (i, 0))]
    in_specs += [_resident((1, d), lambda i: (0, 0)) for _ in groups]
    in_specs += [_resident(w.shape, lambda i: (0, 0)) for w in ws]
    out_specs = [pl.BlockSpec((tm, w.shape[1]), lambda i: (i, 0)) for w in ws]
    out_shape = [jax.ShapeDtypeStruct((n, w.shape[1]), g[2]) for w, g in zip(ws, groups)]
    vmem = 2 * tm * d * 4 + sum(w.size * 2 for w in ws)
    vmem += sum(2 * tm * w.shape[1] * jnp.dtype(g[2]).itemsize for w, g in zip(ws, groups))
    vmem += 4 * tm * max(d, col_chunk) * 4
    return pl.pallas_call(
        functools.partial(_norm_proj_kernel, n_groups=n_groups,
                          scales=tuple(float(g[3]) for g in groups), col_chunk=col_chunk),
        grid=(n // tm,),
        in_specs=in_specs,
        out_specs=out_specs,
        out_shape=out_shape,
        compiler_params=pltpu.CompilerParams(
            dimension_semantics=("arbitrary",), vmem_limit_bytes=_vmem_limit(vmem)),
        name="norm_proj",
    )(x, *gains, *ws)


def _post_mlp_kernel(a_ref, wo_ref, gmix_ref, r_ref, gpre_ref, wup_ref, wdn_ref, gpost_ref, o_ref,
                     *, ff_chunk):
    rows = a_ref.shape[0]
    parts = [slice(0, rows // 2), slice(rows // 2, rows)]
    mixes = [_dot(a_ref[p, :], wo_ref[...]) for p in parts]
    xs = [r_ref[p, :] + m * _rms_scale(m) * gmix_ref[...] for p, m in zip(parts, mixes)]
    hs = [(x * _rms_scale(x) * gpre_ref[...]).astype(BF16) for x in xs]
    d_ff = wup_ref.shape[1]
    ys = [None] * len(parts)
    for c in range(0, d_ff, ff_chunk):
        for idx, h in enumerate(hs):
            u = jnp.maximum(_dot(h, wup_ref[:, c:c + ff_chunk]), 0.0)
            part = _dot((u * u).astype(BF16), wdn_ref[c:c + ff_chunk, :])
            ys[idx] = part if ys[idx] is None else ys[idx] + part
    for p, x, y in zip(parts, xs, ys):
        o_ref[p, :] = x + y * _rms_scale(y) * gpost_ref[...]


def post_mlp(a, w_o, g_mix, resid, g_pre, w_up, w_down, g_post, *, row_tile=512, ff_chunk=1024):
    n, k = a.shape
    d = w_o.shape[1]
    d_ff = w_up.shape[1]
    tm = min(row_tile, n)
    fc = min(ff_chunk, d_ff)
    assert n % tm == 0 and d_ff % fc == 0

    def row_spec(width):
        return pl.BlockSpec((tm, width), lambda i: (i, 0))

    def gain(g):
        return g.reshape(1, d).astype(F32)

    vmem = 2 * tm * k * 2 + 4 * tm * d * 4 + (k * d + 2 * d * d_ff) * 2
    vmem += 3 * tm * fc * 4 + 4 * tm * d * 4
    return pl.pallas_call(
        functools.partial(_post_mlp_kernel, ff_chunk=fc),
        grid=(n // tm,),
        in_specs=[row_spec(k),
                  _resident((k, d), lambda i: (0, 0)),
                  _resident((1, d), lambda i: (0, 0)),
                  row_spec(d),
                  _resident((1, d), lambda i: (0, 0)),
                  _resident((d, d_ff), lambda i: (0, 0)),
                  _resident((d_ff, d), lambda i: (0, 0)),
                  _resident((1, d), lambda i: (0, 0))],
        out_specs=row_spec(d),
        out_shape=jax.ShapeDtypeStruct((n, d), F32),
        compiler_params=pltpu.CompilerParams(
            dimension_semantics=("arbitrary",), vmem_limit_bytes=_vmem_limit(vmem)),
        name="post_mlp",
    )(a, w_o, gain(g_mix), resid, gain(g_pre), w_up, w_down, gain(g_post))


def _conv_silu(x_bf16, w, win_ref):
    n = x_bf16.shape[0]
    x = x_bf16.astype(F32)
    win_ref[V7X_SUBLANES:, :] = x
    acc = x * w[CONV_TAPS - 1:CONV_TAPS, :]
    for s in range(1, CONV_TAPS):
        acc = acc + win_ref[pl.ds(V7X_SUBLANES - s, n), :] * w[CONV_TAPS - 1 - s:CONV_TAPS - s, :]
    win_ref[:V7X_SUBLANES, :] = x[n - V7X_SUBLANES:, :]
    return acc * jax.nn.sigmoid(acc)


def _softplus(x):
    return jnp.maximum(x, 0.0) + jnp.log(1.0 + jnp.exp(-jnp.abs(x)))


def _col_bcast(row, width):
    return jnp.broadcast_to(row, (width, row.shape[1])).T


MASK_INCL, MASK_EYE, MASK_BASE, MASK_MERGE0 = 0, 1, 2, 3


def _gdn_masks(n):
    ri = np.arange(n)[:, None]
    ci = np.arange(n)[None, :]
    masks = [ri >= ci, ri == ci, (ri // GDN_BASE == ci // GDN_BASE) & (ri > ci)]
    size = 2 * GDN_BASE
    while size <= n:
        masks.append((ri // size == ci // size) & (ri // (size // 2) > ci // (size // 2)))
        size *= 2
    return np.stack(masks).astype(np.float32)


def _inv_unit_lower(lows, masks_ref):
    eye = masks_ref[MASK_EYE]
    powers = [low * masks_ref[MASK_BASE] for low in lows]
    invs = [eye - p for p in powers]
    span = 2
    while span < GDN_BASE:
        powers = [_dot(p, p).astype(BF16) for p in powers]
        invs = [_dot(inv, eye + p).astype(BF16) for inv, p in zip(invs, powers)]
        span *= 2
    for level in range(MASK_MERGE0, masks_ref.shape[0]):
        halves = [_dot(inv, low * masks_ref[level]).astype(BF16) for inv, low in zip(invs, lows)]
        invs = [_dot(eye - h, inv).astype(BF16) for h, inv in zip(halves, invs)]
    return invs


def _gdn_head_inputs(hh, head, alog_ref, dtb_ref, q_ref, k_ref, v_ref, cwq_ref, cwk_ref, cwv_ref,
                     ab_ref, masks_ref, bias_ref, tail_ref, head_dim):
    tb = q_ref.shape[0]
    cols = slice(hh * head_dim, (hh + 1) * head_dim)
    q = _conv_silu(q_ref[:, cols], cwq_ref[:, cols], tail_ref.at[0, hh])
    k = _conv_silu(k_ref[:, cols], cwk_ref[:, cols], tail_ref.at[1, hh])
    v = _conv_silu(v_ref[:, cols], cwv_ref[:, cols], tail_ref.at[2, hh])
    q = q * (lax.rsqrt(jnp.sum(q * q, axis=-1, keepdims=True) + EPS) * head_dim ** -0.5)
    k = k * lax.rsqrt(jnp.sum(k * k, axis=-1, keepdims=True) + EPS)

    ab = ab_ref[hh]
    a_log = jnp.zeros((1, tb), F32) + alog_ref[head]
    g_row = -jnp.exp(a_log) * _softplus(ab[0:1] + dtb_ref[head])
    beta_row = jax.nn.sigmoid(ab[1:2])
    g_col = _col_bcast(g_row, head_dim)
    beta = _col_bcast(beta_row, head_dim)
    tril16 = masks_ref[MASK_INCL]
    g_hi = g_col.astype(BF16)
    g_lo = (g_col - g_hi.astype(F32)).astype(BF16)
    gc2 = _dot(tril16, jnp.concatenate([g_hi, g_lo], axis=1))
    gc = gc2[:, :head_dim] + gc2[:, head_dim:]
    gc_row = gc.T[0:1, :]
    gc_last = gc[tb - 1:tb, :]
    diff = jnp.concatenate([gc] * (tb // head_dim), axis=1) - gc_row
    decay = jnp.exp(diff + bias_ref[...])

    kb = k * beta
    qk = _dot_nt(jnp.concatenate([q, kb], axis=0).astype(BF16), k.astype(BF16))
    e_gc = jnp.exp(gc)
    return dict(
        attn16=(qk[:tb] * decay).astype(BF16),
        low16=(qk[tb:] * decay).astype(BF16),
        wu_rhs16=jnp.concatenate([kb * e_gc, v * beta], axis=1).astype(BF16),
        qg16=(q * e_gc).astype(BF16),
        kg16=(k * jnp.exp(gc_last - gc)).astype(BF16),
        g_last=jnp.exp(gc_last))


def _gdn_kernel(alog_ref, dtb_ref, q_ref, k_ref, v_ref, gate_ref, cwq_ref, cwk_ref, cwv_ref,
                ab_ref, og_ref, masks_ref, bias_ref, o_ref, state_ref, tail_ref,
                *, heads_per_step, head_dim):
    hblk = pl.program_id(1)
    tb = q_ref.shape[0]
    heads = range(heads_per_step)

    @pl.when(pl.program_id(2) == 0)
    def _():
        state_ref[...] = jnp.zeros_like(state_ref)
        tail_ref[...] = jnp.zeros_like(tail_ref)

    hd = [_gdn_head_inputs(hh, hblk * heads_per_step + hh, alog_ref, dtb_ref, q_ref, k_ref, v_ref,
                           cwq_ref, cwk_ref, cwv_ref, ab_ref, masks_ref, bias_ref, tail_ref,
                           head_dim)
          for hh in heads]
    t_mats = _inv_unit_lower([h["low16"] for h in hd], masks_ref)
    wus = [_dot(t, h["wu_rhs16"]) for t, h in zip(t_mats, hd)]
    states = [state_ref[hh] for hh in heads]
    ws_qs = [_dot(jnp.concatenate([wu[:, :head_dim].astype(BF16), h["qg16"]], axis=0),
                  s.astype(BF16)) for wu, h, s in zip(wus, hd, states)]
    v_new16 = [(wu[:, head_dim:] - wq[:tb]).astype(BF16) for wu, wq in zip(wus, ws_qs)]
    outs = [wq[tb:] + _dot(h["attn16"], vn) for wq, h, vn in zip(ws_qs, hd, v_new16)]
    for hh in heads:
        state_ref[hh] = states[hh] * hd[hh]["g_last"] + _dot_tn(hd[hh]["kg16"], v_new16[hh])
    for hh in heads:
        cols = slice(hh * head_dim, (hh + 1) * head_dim)
        gate = gate_ref[:, cols].astype(F32)
        o = outs[hh]
        o = o * _rms_scale(o) * og_ref[...] * (gate * jax.nn.sigmoid(gate))
        o_ref[:, cols] = o.astype(o_ref.dtype)


def gdn(proj, conv_w, ab, a_log, dt_bias, out_gain, *, n_heads, head_dim, heads_per_step=8):
    b, t, _ = proj.shape
    tb = min(GDN_BLOCK, t)
    assert t % tb == 0 and tb % head_dim == 0 and n_heads % heads_per_step == 0
    hb = heads_per_step
    n_hblk = n_heads // hb
    wblk = hb * head_dim
    masks_np = _gdn_masks(tb)
    masks = jnp.asarray(masks_np, dtype=BF16)
    bias = jnp.asarray(np.where(masks_np[MASK_INCL] > 0, 0.0, -np.inf), dtype=F32)
    vmem = masks.size * 2 + bias.size * 4 + 2 * 5 * tb * wblk * 2 + 40 * hb * tb * tb * 4

    def act_spec(section):
        return pl.BlockSpec((None, tb, wblk), lambda bi, hi, ti: (bi, ti, section * n_hblk + hi))

    def cw_spec(section):
        return pl.BlockSpec((CONV_TAPS, wblk), lambda bi, hi, ti: (0, section * n_hblk + hi))

    smem = pl.BlockSpec(memory_space=pltpu.SMEM)
    return pl.pallas_call(
        functools.partial(_gdn_kernel, heads_per_step=hb, head_dim=head_dim),
        grid=(b, n_hblk, t // tb),
        in_specs=[smem, smem,
                  act_spec(0), act_spec(1), act_spec(2), act_spec(3),
                  cw_spec(0), cw_spec(1), cw_spec(2),
                  pl.BlockSpec((None, hb, 2, tb), lambda bi, hi, ti: (bi, hi, 0, ti)),
                  pl.BlockSpec((1, head_dim), lambda bi, hi, ti: (0, 0)),
                  _resident(masks.shape, lambda bi, hi, ti: (0, 0, 0)),
                  _resident(bias.shape, lambda bi, hi, ti: (0, 0))],
        out_specs=pl.BlockSpec((None, tb, wblk), lambda bi, hi, ti: (bi, ti, hi)),
        out_shape=jax.ShapeDtypeStruct((b, t, n_heads * head_dim), BF16),
        scratch_shapes=[pltpu.VMEM((hb, head_dim, head_dim), F32),
                        pltpu.VMEM((3, hb, V7X_SUBLANES + tb, head_dim), F32)],
        compiler_params=pltpu.CompilerParams(
            dimension_semantics=("arbitrary", "arbitrary", "arbitrary"),
            vmem_limit_bytes=_vmem_limit(vmem)),
        name="gdn",
    )(a_log.astype(F32), dt_bias.astype(F32), proj, proj, proj, proj,
      conv_w, conv_w, conv_w, ab, out_gain.reshape(1, head_dim).astype(F32), masks, bias)


def _sb_kernel(q_ref, k_ref, v_ref, o_ref, l1m_ref, lb_ref, rs_ref, a_ref, csum_ref, acc_ref,
               *, block, heads_per_step, head_dim):
    i = pl.program_id(2)
    ri = lax.broadcasted_iota(jnp.int32, (block, block), 0)
    ci = lax.broadcasted_iota(jnp.int32, (block, block), 1)
    upper = (ri > ci).astype(BF16)
    before = ci < ri
    heads = [slice(hh * head_dim, (hh + 1) * head_dim) for hh in range(heads_per_step)]

    def rows(blk):
        return pl.ds(pl.multiple_of(blk * block, block), block)

    def stage_a(blk, mask):
        for hh, cols in enumerate(heads):
            z = _dot_nt(q_ref[:, cols], k_ref[rows(blk), cols])
            sp = jnp.log2(1.0 + jnp.exp2(-jnp.abs(z)))
            log_beta = jnp.minimum(z, 0.0) - sp
            log_1m = log_beta - z
            if mask is not None:
                log_1m = jnp.where(mask, log_1m, 0.0)
                log_beta = jnp.where(mask, log_beta, -jnp.inf)
            l1m_ref[hh] = log_1m.astype(BF16)
            lb_ref[hh] = log_beta
            rs_ref[hh] = jnp.broadcast_to(jnp.sum(log_1m, axis=-1, keepdims=True),
                                          rs_ref.shape[1:])

    def stage_b():
        for hh in range(heads_per_step):
            csum = csum_ref[hh]
            tail = (_dot(l1m_ref[hh], upper)
                    + jnp.concatenate([csum] * (block // V7X_LANES), axis=1))
            a_ref[hh] = jnp.exp2(lb_ref[hh] + tail).astype(BF16)
            csum_ref[hh] = csum + rs_ref[hh]

    def stage_c(blk):
        for hh, cols in enumerate(heads):
            acc_ref[hh] += _dot(a_ref[hh], v_ref[rows(blk), cols])

    acc_ref[...] = jnp.zeros_like(acc_ref)
    csum_ref[...] = jnp.zeros_like(csum_ref)

    @pl.when(i == 0)
    def _():
        stage_a(0, before)
        stage_b()
        stage_c(0)

    @pl.when(i >= 1)
    def _():
        stage_a(i, before)
        stage_b()
        stage_a(i - 1, None)

    def body(t, carry):
        blk = i - t
        stage_c(blk + 2)
        stage_b()
        stage_a(blk, None)
        return carry

    lax.fori_loop(2, i + 1, body, 0)

    @pl.when(i >= 1)
    def _():
        stage_c(1)
        stage_b()
        stage_c(0)

    for hh, cols in enumerate(heads):
        o_ref[:, cols] = acc_ref[hh].astype(o_ref.dtype)


def sb_attn(q, kv, *, n_heads, head_dim, block=256, heads_per_step=8):
    b, t, _ = q.shape
    blk = min(block, t)
    hp = heads_per_step
    assert t % blk == 0 and n_heads % hp == 0 and head_dim == V7X_LANES
    n_hblk = n_heads // hp
    wblk = hp * head_dim
    tile32 = hp * blk * blk * 4
    scratch_bytes = 2 * tile32 + 3 * hp * blk * V7X_LANES * 4
    vmem = 2 * t * wblk * 2 + 4 * blk * wblk * 2 + scratch_bytes + 4 * tile32

    def seq_spec(index_map):
        return pl.BlockSpec((None, t, wblk), index_map, pipeline_mode=pl.Buffered(1))

    return pl.pallas_call(
        functools.partial(_sb_kernel, block=blk, heads_per_step=hp, head_dim=head_dim),
        grid=(b, n_hblk, t // blk),
        in_specs=[pl.BlockSpec((None, blk, wblk), lambda bi, hi, qi: (bi, qi, hi)),
                  seq_spec(lambda bi, hi, qi: (bi, 0, hi)),
                  seq_spec(lambda bi, hi, qi: (bi, 0, n_hblk + hi))],
        out_specs=pl.BlockSpec((None, blk, wblk), lambda bi, hi, qi: (bi, qi, hi)),
        out_shape=jax.ShapeDtypeStruct((b, t, n_heads * head_dim), BF16),
        scratch_shapes=[pltpu.VMEM((hp, blk, blk), BF16),
                        pltpu.VMEM((hp, blk, blk), F32),
                        pltpu.VMEM((hp, blk, V7X_LANES), F32),
                        pltpu.VMEM((hp, blk, blk), BF16),
                        pltpu.VMEM((hp, blk, V7X_LANES), F32),
                        pltpu.VMEM((hp, blk, head_dim), F32)],
        compiler_params=pltpu.CompilerParams(
            dimension_semantics=("arbitrary", "arbitrary", "arbitrary"),
            vmem_limit_bytes=_vmem_limit(vmem)),
        name="sb_attn",
    )(q, kv, kv)


def kernel(x, mix_pre_gain, mix_post_gain, mlp_pre_gain, mlp_post_gain, mlp_w_up, mlp_w_down,
           gdn_w_in, gdn_conv_w, gdn_a_log, gdn_dt_bias, gdn_out_gain, gdn_w_out,
           kv_gain, w_kv, sb_w_q, sb_w_o):
    b, t, d = x.shape
    n = b * t
    n_a = gdn_w_in.shape[0]
    depth = mix_pre_gain.shape[0]
    gdn_heads = gdn_a_log.shape[1]
    gdn_dim = gdn_out_gain.shape[1]
    gdn_width = gdn_heads * gdn_dim
    sb_width = sb_w_q.shape[2]
    sb_dim = gdn_dim
    sb_heads = sb_width // sb_dim
    assert gdn_w_in.shape[2] == 4 * gdn_width + 2 * gdn_heads

    xs = x.reshape(n, d)
    kv = None
    for layer in range(depth):
        if layer < n_a:
            w_in = gdn_w_in[layer]
            w_main = w_in[:, :4 * gdn_width].astype(BF16)
            w_small = jnp.concatenate(
                [w_in[:, 4 * gdn_width + gdn_heads:], w_in[:, 4 * gdn_width:4 * gdn_width + gdn_heads],
                 jnp.zeros((d, V7X_LANES - 2 * gdn_heads), w_in.dtype)], axis=1).astype(BF16)
            pre = mix_pre_gain[layer]
            proj, small = norm_proj(xs, [(pre, w_main, BF16, 1.0), (pre, w_small, F32, 1.0)])
            small = small[:, :2 * gdn_heads].reshape(b, t, 2, gdn_heads)
            ab = jnp.transpose(small, (0, 3, 2, 1))
            o = gdn(proj.reshape(b, t, 4 * gdn_width), gdn_conv_w[layer], ab,
                    gdn_a_log[layer], gdn_dt_bias[layer], gdn_out_gain[layer],
                    n_heads=gdn_heads, head_dim=gdn_dim)
            w_o = gdn_w_out[layer]
        else:
            bl = layer - n_a
            q_scale = sb_dim ** -0.5 * math.log2(math.e)
            groups = [(mix_pre_gain[layer], sb_w_q[bl].astype(BF16), BF16, q_scale)]
            if kv is None:
                groups.append((kv_gain, w_kv.astype(BF16), BF16, 1.0))
                q, kv = norm_proj(xs, groups)
                kv = kv.reshape(b, t, 2 * sb_width)
            else:
                (q,) = norm_proj(xs, groups)
            o = sb_attn(q.reshape(b, t, sb_width), kv, n_heads=sb_heads, head_dim=sb_dim)
            w_o = sb_w_o[bl]
        xs = post_mlp(o.reshape(n, w_o.shape[0]), w_o.astype(BF16), mix_post_gain[layer], xs,
                      mlp_pre_gain[layer], mlp_w_up[layer].astype(BF16),
                      mlp_w_down[layer].astype(BF16), mlp_post_gain[layer])
    return xs.reshape(b, t, d)
```
